```python
import jax, jax.numpy as jnp
from jax import lax
import numpy as np

D_MODEL = 1024
BATCH = 4
SEQ = 4096
DEPTH = 1
DEC_BATCH = 128
DEC_SEQ = 4
PAST_LEN = 8192
PAGE_SIZE = 128

MIX_WIDTH = D_MODEL
N_HEADS = 8
V_DIM = MIX_WIDTH // 2 // N_HEADS
NOPE_DIM = 64
ROPE_DIM = 32
Q_RANK = D_MODEL // 4
KV_RANK = D_MODEL // 8
LRU_WIDTH = MIX_WIDTH - N_HEADS * V_DIM
LRU_BLOCKS = 8
LRU_BLOCK_DIM = LRU_WIDTH // LRU_BLOCKS
CONV_WIDTH = 4
LRU_C = 8.0
MEM_TOKENS = 256
MEM_HEADS = 4
MEM_HEAD_DIM = D_MODEL // MEM_HEADS
N_EXPERTS = 32
TOP_K = 4
EXPERT_FF = D_MODEL
SWIGLU_LIMIT = 7.0
SWIGLU_ALPHA = 1.702
MOE_BLOCK = 128
Q_BLOCK = 128
ROPE_THETA = 10000.0
EPS = 1e-6
MLA_SCALE = (NOPE_DIM + ROPE_DIM) ** -0.5
IN_COLS = Q_RANK + KV_RANK + ROPE_DIM + 2 * LRU_WIDTH

kernel_name = 'hymba_mla_rglru_moe_decode_step'


def _rmsnorm(x, g):
    xf = x.astype(jnp.float32)
    y = xf * lax.rsqrt(jnp.mean(xf * xf, axis=-1, keepdims=True) + EPS)
    return (y * g.astype(jnp.float32)).astype(x.dtype)


def _rope(x, pos):
    half = ROPE_DIM // 2
    inv = ROPE_THETA ** (-jnp.arange(half, dtype=jnp.float32) / half)
    ang = pos.astype(jnp.float32)[:, None] * inv[None, :]
    if x.ndim == 4:
        ang = ang[:, None, :]
    cos, sin = jnp.cos(ang), jnp.sin(ang)
    x1 = x[..., :half].astype(jnp.float32)
    x2 = x[..., half:].astype(jnp.float32)
    return jnp.concatenate([x1 * cos - x2 * sin, x1 * sin + x2 * cos], axis=-1).astype(x.dtype)


def _mixer_inputs(hn, pos, lp):
    b, t = hn.shape[:2]
    proj = hn @ lp['w_in']
    c0 = Q_RANK
    c1 = c0 + KV_RANK
    c2 = c1 + ROPE_DIM
    c3 = c2 + LRU_WIDTH
    q_in, ckv, krope = proj[..., :c0], proj[..., c0:c1], proj[..., c1:c2]
    xb, gb = proj[..., c2:c3], proj[..., c3:]
    q = (_rmsnorm(q_in, lp['norm_q_lat']) @ lp['w_q_b']).reshape(b, t, N_HEADS, NOPE_DIM + ROPE_DIM)
    q_nope, q_rope = q[..., :NOPE_DIM], _rope(q[..., NOPE_DIM:], pos)
    ckv = _rmsnorm(ckv, lp['norm_kv_lat'])
    krope = _rope(krope, pos)
    w_kv = lp['w_kv_b'].reshape(KV_RANK, N_HEADS, NOPE_DIM + V_DIM)
    q_lat = jnp.einsum('bthn,rhn->bthr', q_nope, w_kv[..., :NOPE_DIM])
    return q_lat, q_rope, ckv, krope, xb, gb


def _mla_scores(q_lat, q_rope, ckv, krope):
    s = jnp.einsum('bqhr,btr->bhqt', q_lat, ckv) + jnp.einsum('bqhd,btd->bhqt', q_rope, krope)
    return s.astype(jnp.float32) * MLA_SCALE


def _mla_prompt(q_lat, q_rope, ckv, krope):
    b, s = q_lat.shape[:2]
    n_blocks = s // Q_BLOCK
    k_pos = jnp.arange(s)

    def one_block(i):
        q0 = i * Q_BLOCK
        ql = lax.dynamic_slice_in_dim(q_lat, q0, Q_BLOCK, axis=1)
        qr = lax.dynamic_slice_in_dim(q_rope, q0, Q_BLOCK, axis=1)
        sc = _mla_scores(ql, qr, ckv, krope)
        q_pos = q0 + jnp.arange(Q_BLOCK)
        sc = jnp.where(k_pos[None, :] <= q_pos[:, None], sc, -jnp.inf)
        p = jax.nn.softmax(sc, axis=-1)
        return jnp.einsum('bhqt,btr->bqhr', p, ckv)

    out = lax.map(one_block, jnp.arange(n_blocks))
    return jnp.moveaxis(out, 0, 1).reshape(b, s, N_HEADS, KV_RANK).astype(ckv.dtype)


def _mla_sample(q_lat, q_rope, ckv, krope, past_ckv, past_krope):
    t = q_lat.shape[1]
    n_past = past_ckv.shape[1]
    s_past = _mla_scores(q_lat, q_rope, past_ckv, past_krope)
    s_new = _mla_scores(q_lat, q_rope, ckv, krope)
    causal = jnp.tril(jnp.ones((t, t), dtype=bool))
    s_new = jnp.where(causal, s_new, -jnp.inf)
    p = jax.nn.softmax(jnp.concatenate([s_past, s_new], axis=-1), axis=-1)
    o = (jnp.einsum('bhqt,btr->bqhr', p[..., :n_past], past_ckv)
         + jnp.einsum('bhqt,btr->bqhr', p[..., n_past:], ckv))
    return o.astype(ckv.dtype)


def _mla_out(o_lat, lp):
    b, t = o_lat.shape[:2]
    w_uv = lp['w_kv_b'].reshape(KV_RANK, N_HEADS, NOPE_DIM + V_DIM)[..., NOPE_DIM:]
    return jnp.einsum('bthr,rhv->bthv', o_lat, w_uv).reshape(b, t, N_HEADS * V_DIM)


def _lin_comb(c1, c2):
    a1, b1 = c1
    a2, b2 = c2
    return a1 * a2, a2 * b1 + b2


def _rglru(xb, gb, conv_buf, h0, lp):
    b, t, _ = xb.shape
    xpad = jnp.concatenate([conv_buf.astype(xb.dtype), xb], axis=1)
    conv_w = lp['conv_w']
    xc = sum(xpad[:, j:j + t] * conv_w[j] for j in range(CONV_WIDTH)) + lp['conv_b']
    xblk = xc.reshape(b, t, LRU_BLOCKS, LRU_BLOCK_DIM)
    gate_x = jax.nn.sigmoid(jnp.einsum('btnd,nde->btne', xblk, lp['w_gate_x']) + lp['b_gate_x']).reshape(b, t, LRU_WIDTH)
    gate_a = jax.nn.sigmoid(jnp.einsum('btnd,nde->btne', xblk, lp['w_gate_a']) + lp['b_gate_a']).reshape(b, t, LRU_WIDTH)
    log_a = -LRU_C * gate_a.astype(jnp.float32) * jax.nn.softplus(-lp['lru_lambda'].astype(jnp.float32))
    a = jnp.exp(log_a)
    u = jnp.sqrt(-jnp.expm1(2.0 * log_a)) * (gate_x * xc).astype(jnp.float32)
    u = u.at[:, 0].add(a[:, 0] * h0.astype(jnp.float32))
    _, h = lax.associative_scan(_lin_comb, (a, u), axis=1)
    y = (h * jax.nn.gelu(gb.astype(jnp.float32))).astype(xb.dtype)
    return y, xpad[:, -(CONV_WIDTH - 1):], h[:, -1].astype(xb.dtype)


def _mix_out(att, rec, lp):
    cat = jnp.concatenate([_rmsnorm(att, lp['norm_att_out']), _rmsnorm(rec, lp['norm_rec_out'])], axis=-1)
    return cat @ lp['w_o']


def _mem_kv(mem, lp):
    b, m, _ = mem.shape
    kv = _rmsnorm(mem, lp['norm_memtok']) @ lp['w_mem_kv']
    k = kv[..., :D_MODEL].reshape(b, m, MEM_HEADS, MEM_HEAD_DIM)
    v = kv[..., D_MODEL:].reshape(b, m, MEM_HEADS, MEM_HEAD_DIM)
    return k, v


def _mem_attend(hn, mk, mv, lp):
    b, t, _ = hn.shape
    q = (hn @ lp['w_mem_q']).reshape(b, t, MEM_HEADS, MEM_HEAD_DIM)
    s = jnp.einsum('bthd,bmhd->bhtm', q, mk).astype(jnp.float32) * (MEM_HEAD_DIM ** -0.5)
    p = jax.nn.softmax(s, axis=-1)
    o = jnp.einsum('bhtm,bmhd->bthd', p, mv).astype(hn.dtype).reshape(b, t, D_MODEL)
    return o @ lp['w_mem_o']


def _moe(hn, lp):
    b, t, d = hn.shape
    x2 = hn.reshape(b * t, d)
    n = b * t
    nk = n * TOP_K
    logits = (x2 @ lp['w_router'] + lp['b_router']).astype(jnp.float32)
    top_val, top_idx = lax.top_k(logits, TOP_K)
    gates = jax.nn.softmax(top_val, axis=-1)
    e_flat = top_idx.reshape(-1)
    tok_flat = jnp.repeat(jnp.arange(n), TOP_K)
    g_flat = gates.reshape(-1)
    order = jnp.argsort(e_flat)
    se, stok, sg = e_flat[order], tok_flat[order], g_flat[order]
    counts = jnp.bincount(e_flat, length=N_EXPERTS)
    starts = jnp.cumsum(counts) - counts
    padded = ((counts + MOE_BLOCK - 1) // MOE_BLOCK) * MOE_BLOCK
    pends = jnp.cumsum(padded)
    pstarts = pends - padded
    dest = pstarts[se] + (jnp.arange(nk) - starts[se])
    n_blocks = -(-nk // MOE_BLOCK) + N_EXPERTS
    buf = jnp.zeros((n_blocks * MOE_BLOCK, d), x2.dtype).at[dest].set(x2[stok])
    block_e = jnp.clip(jnp.searchsorted(pends, jnp.arange(n_blocks) * MOE_BLOCK, side='right'), 0, N_EXPERTS - 1)
    w_gu, b_gu, w_dn, b_dn = lp['w_gate_up'], lp['b_gate_up'], lp['w_down'], lp['b_down']

    def expert_block(args):
        xb, e = args
        gu = (xb @ w_gu[e] + b_gu[e]).astype(jnp.float32)
        gate = jnp.minimum(gu[:, :EXPERT_FF], SWIGLU_LIMIT)
        lin = jnp.clip(gu[:, EXPERT_FF:], -SWIGLU_LIMIT, SWIGLU_LIMIT)
        act = ((lin + 1.0) * gate * jax.nn.sigmoid(SWIGLU_ALPHA * gate)).astype(xb.dtype)
        return act @ w_dn[e] + b_dn[e]

    out_buf = lax.map(expert_block, (buf.reshape(n_blocks, MOE_BLOCK, d), block_e)).reshape(-1, d)
    y_assign = out_buf[dest].astype(jnp.float32) * sg[:, None]
    y = jax.ops.segment_sum(y_assign, stok, num_segments=n)
    return y.astype(hn.dtype).reshape(b, t, d)


def _layer_prompt(x, mem, lp):
    b, s, _ = x.shape
    pos = jnp.arange(s)
    hn = _rmsnorm(x, lp['norm_mix'])
    q_lat, q_rope, ckv, krope, xb, gb = _mixer_inputs(hn, pos, lp)
    att = _mla_out(_mla_prompt(q_lat, q_rope, ckv, krope), lp)
    rec, conv_buf, h_last = _rglru(xb, gb, jnp.zeros((b, CONV_WIDTH - 1, LRU_WIDTH), xb.dtype),
                                   jnp.zeros((b, LRU_WIDTH), xb.dtype), lp)
    x = x + _mix_out(att, rec, lp).astype(x.dtype)
    mk, mv = _mem_kv(mem, lp)
    x = x + _mem_attend(_rmsnorm(x, lp['norm_mem_attn']), mk, mv, lp).astype(x.dtype)
    x = x + _moe(_rmsnorm(x, lp['norm_moe']), lp)
    return x, ckv, krope, conv_buf, h_last, mk, mv


def _layer_sample(x, page_table, pool_ckv, pool_krope, conv_buf, h0, mk, mv, lp):
    b, t, _ = x.shape
    n_past = page_table.shape[1] * PAGE_SIZE
    pos = n_past + jnp.arange(t)
    hn = _rmsnorm(x, lp['norm_mix'])
    q_lat, q_rope, ckv, krope, xb, gb = _mixer_inputs(hn, pos, lp)
    past_ckv = pool_ckv[page_table].reshape(b, n_past, KV_RANK)
    past_krope = pool_krope[page_table].reshape(b, n_past, ROPE_DIM)
    att = _mla_out(_mla_sample(q_lat, q_rope, ckv, krope, past_ckv, past_krope), lp)
    rec, new_conv, h_last = _rglru(xb, gb, conv_buf, h0, lp)
    x = x + _mix_out(att, rec, lp).astype(x.dtype)
    x = x + _mem_attend(_rmsnorm(x, lp['norm_mem_attn']), mk, mv, lp).astype(x.dtype)
    x = x + _moe(_rmsnorm(x, lp['norm_moe']), lp)
    return x, ckv, krope, new_conv, h_last


def setup_inputs(seed: int = 0) -> dict:
    key = jax.random.key(seed)
    ks = iter(jax.random.split(key, 48))
    f32 = jnp.float32

    def nrm(shape, scale):
        return jax.random.normal(next(ks), shape, f32) * scale

    def gain(shape):
        return 1.0 + 0.01 * jax.random.normal(next(ks), shape, f32)

    n_pages = PAST_LEN // PAGE_SIZE
    n_used = DEC_BATCH * n_pages
    n_pool = n_used + n_used // 4
    page_table = jax.random.permutation(next(ks), n_pool)[:n_used].reshape(DEC_BATCH, n_pages).astype(jnp.int32)
    u = jax.random.uniform(next(ks), (DEPTH, LRU_WIDTH), f32, minval=0.9, maxval=0.999)
    inp = {}
    inp['x_prompt'] = nrm((BATCH, SEQ, D_MODEL), 1.0)
    inp['x_sample'] = nrm((DEC_BATCH, DEC_SEQ, D_MODEL), 1.0)
    inp['mem_prompt'] = nrm((BATCH, MEM_TOKENS, D_MODEL), 1.0)
    inp['cache_ckv'] = nrm((DEPTH, n_pool, PAGE_SIZE, KV_RANK), 1.0)
    inp['cache_krope'] = nrm((DEPTH, n_pool, PAGE_SIZE, ROPE_DIM), 1.0)
    inp['cache_conv'] = nrm((DEPTH, DEC_BATCH, CONV_WIDTH - 1, LRU_WIDTH), 1.0)
    inp['state_rglru'] = nrm((DEPTH, DEC_BATCH, LRU_WIDTH), 0.5)
    inp['cache_mem_k'] = nrm((DEPTH, DEC_BATCH, MEM_TOKENS, MEM_HEADS, MEM_HEAD_DIM), 1.0)
    inp['cache_mem_v'] = nrm((DEPTH, DEC_BATCH, MEM_TOKENS, MEM_HEADS, MEM_HEAD_DIM), 1.0)
    inp['page_table'] = page_table
    inp['w_in'] = nrm((DEPTH, D_MODEL, IN_COLS), D_MODEL ** -0.5)
    inp['norm_mix'] = gain((DEPTH, D_MODEL))
    inp['norm_q_lat'] = gain((DEPTH, Q_RANK))
    inp['w_q_b'] = nrm((DEPTH, Q_RANK, N_HEADS * (NOPE_DIM + ROPE_DIM)), Q_RANK ** -0.5)
    inp['norm_kv_lat'] = gain((DEPTH, KV_RANK))
    inp['w_kv_b'] = nrm((DEPTH, KV_RANK, N_HEADS * (NOPE_DIM + V_DIM)), KV_RANK ** -0.5)
    inp['conv_w'] = nrm((DEPTH, CONV_WIDTH, LRU_WIDTH), CONV_WIDTH ** -0.5)
    inp['conv_b'] = nrm((DEPTH, LRU_WIDTH), 0.01)
    inp['w_gate_x'] = nrm((DEPTH, LRU_BLOCKS, LRU_BLOCK_DIM, LRU_BLOCK_DIM), LRU_BLOCK_DIM ** -0.5)
    inp['b_gate_x'] = nrm((DEPTH, LRU_BLOCKS, LRU_BLOCK_DIM), 0.01)
    inp['w_gate_a'] = nrm((DEPTH, LRU_BLOCKS, LRU_BLOCK_DIM, LRU_BLOCK_DIM), LRU_BLOCK_DIM ** -0.5)
    inp['b_gate_a'] = nrm((DEPTH, LRU_BLOCKS, LRU_BLOCK_DIM), 0.01)
    inp['lru_lambda'] = jnp.log(u) - jnp.log1p(-u)
    inp['norm_att_out'] = gain((DEPTH, N_HEADS * V_DIM))
    inp['norm_rec_out'] = gain((DEPTH, LRU_WIDTH))
    inp['w_o'] = nrm((DEPTH, MIX_WIDTH, D_MODEL), MIX_WIDTH ** -0.5)
    inp['norm_mem_attn'] = gain((DEPTH, D_MODEL))
    inp['norm_memtok'] = gain((DEPTH, D_MODEL))
    inp['w_mem_q'] = nrm((DEPTH, D_MODEL, D_MODEL), D_MODEL ** -0.5)
    inp['w_mem_kv'] = nrm((DEPTH, D_MODEL, 2 * D_MODEL), D_MODEL ** -0.5)
    inp['w_mem_o'] = nrm((DEPTH, D_MODEL, D_MODEL), D_MODEL ** -0.5)
    inp['norm_moe'] = gain((DEPTH, D_MODEL))
    inp['w_router'] = nrm((DEPTH, D_MODEL, N_EXPERTS), D_MODEL ** -0.5)
    inp['b_router'] = nrm((DEPTH, N_EXPERTS), 0.01)
    inp['w_gate_up'] = nrm((DEPTH, N_EXPERTS, D_MODEL, 2 * EXPERT_FF), D_MODEL ** -0.5)
    inp['b_gate_up'] = nrm((DEPTH, N_EXPERTS, 2 * EXPERT_FF), 0.01)
    inp['w_down'] = nrm((DEPTH, N_EXPERTS, EXPERT_FF, D_MODEL), EXPERT_FF ** -0.5)
    inp['b_down'] = nrm((DEPTH, N_EXPERTS, D_MODEL), 0.01)
    inp['norm_final'] = gain((D_MODEL,))
    return inp


def reference(x_prompt, x_sample, mem_prompt, cache_ckv, cache_krope, cache_conv, state_rglru,
              cache_mem_k, cache_mem_v, page_table, w_in, norm_mix, norm_q_lat, w_q_b, norm_kv_lat,
              w_kv_b, conv_w, conv_b, w_gate_x, b_gate_x, w_gate_a, b_gate_a, lru_lambda,
              norm_att_out, norm_rec_out, w_o, norm_mem_attn, norm_memtok, w_mem_q, w_mem_kv, w_mem_o,
              norm_moe, w_router, b_router, w_gate_up, b_gate_up, w_down, b_down, norm_final):
    yp, ys = x_prompt, x_sample
    ckv_p, krope_p, conv_p, h_p, mk_p, mv_p = [], [], [], [], [], []
    ckv_s, krope_s, conv_s, h_s = [], [], [], []
    for l in range(DEPTH):
        lp = dict(w_in=w_in[l], norm_mix=norm_mix[l], norm_q_lat=norm_q_lat[l], w_q_b=w_q_b[l],
                  norm_kv_lat=norm_kv_lat[l], w_kv_b=w_kv_b[l], conv_w=conv_w[l], conv_b=conv_b[l],
                  w_gate_x=w_gate_x[l], b_gate_x=b_gate_x[l], w_gate_a=w_gate_a[l], b_gate_a=b_gate_a[l],
                  lru_lambda=lru_lambda[l], norm_att_out=norm_att_out[l], norm_rec_out=norm_rec_out[l],
                  w_o=w_o[l], norm_mem_attn=norm_mem_attn[l], norm_memtok=norm_memtok[l],
                  w_mem_q=w_mem_q[l], w_mem_kv=w_mem_kv[l], w_mem_o=w_mem_o[l], norm_moe=norm_moe[l],
                  w_router=w_router[l], b_router=b_router[l], w_gate_up=w_gate_up[l],
                  b_gate_up=b_gate_up[l], w_down=w_down[l], b_down=b_down[l])
        yp, c1, c2, c3, c4, c5, c6 = _layer_prompt(yp, mem_prompt, lp)
        ckv_p.append(c1); krope_p.append(c2); conv_p.append(c3); h_p.append(c4); mk_p.append(c5); mv_p.append(c6)
        ys, s1, s2, s3, s4 = _layer_sample(ys, page_table, cache_ckv[l], cache_krope[l], cache_conv[l],
                                           state_rglru[l], cache_mem_k[l], cache_mem_v[l], lp)
        ckv_s.append(s1); krope_s.append(s2); conv_s.append(s3); h_s.append(s4)
    y_prompt = _rmsnorm(yp, norm_final)
    y_sample = _rmsnorm(ys, norm_final)
    new_ckv_prompt = jnp.stack(ckv_p)
    new_krope_prompt = jnp.stack(krope_p)
    new_conv_prompt = jnp.stack(conv_p)
    new_rglru_prompt = jnp.stack(h_p)
    new_mem_k_prompt = jnp.stack(mk_p)
    new_mem_v_prompt = jnp.stack(mv_p)
    new_ckv_sample = jnp.stack(ckv_s)
    new_krope_sample = jnp.stack(krope_s)
    new_conv_sample = jnp.stack(conv_s)
    new_rglru_sample = jnp.stack(h_s)
    return (y_prompt, y_sample, new_ckv_prompt, new_krope_prompt, new_conv_prompt, new_rglru_prompt,
            new_mem_k_prompt, new_mem_v_prompt, new_ckv_sample, new_krope_sample, new_conv_sample,
            new_rglru_sample)
```

```python
import functools

import jax
import jax.numpy as jnp
from jax import lax
from jax.experimental import pallas as pl
from jax.experimental.pallas import tpu as pltpu

F32 = jnp.float32
BF16 = jnp.bfloat16

N_HEADS = 8
NOPE_DIM = 64
V_DIM = 64
ROPE_DIM = 32
ROPE_HALF = ROPE_DIM // 2
KV_RANK = 128
Q_RANK = 256
LRU_WIDTH = 512
LRU_BLOCKS = 8
CONV_WIDTH = 4
LRU_C = 8.0
MEM_HEADS = 4
MEM_HEAD_DIM = 256
N_EXPERTS = 32
TOP_K = 4
EXPERT_FF = 1024
SWIGLU_LIMIT = 7.0
SWIGLU_ALPHA = 1.702
ROPE_THETA = 10000.0
EPS = 1e-6
MLA_SCALE = (NOPE_DIM + ROPE_DIM) ** -0.5
PAGE_SIZE = 128

LANES = 128
SUBLANES = 8
HEADS_PER_LANE_GROUP = LANES // ROPE_DIM
QK_WIDTH = KV_RANK + LANES
NEG_BIG = -1e30
VMEM_LIMIT = 56 * 1024 * 1024

TOKEN_TILE = 256
ATTN_Q_TILE = 128
ATTN_K_TILE = 512
LRU_TIME_TILE = 256
MOE_TILE = 256
MEM_ITEMS = 4
TOK_PAD = SUBLANES


def _params(sem, vmem=VMEM_LIMIT):
    return pltpu.CompilerParams(dimension_semantics=sem, vmem_limit_bytes=vmem)


def _rms(x, g):
    return x * lax.rsqrt(jnp.mean(x * x, axis=-1, keepdims=True) + EPS) * g


def _dot(a, b):
    return jnp.dot(a, b, preferred_element_type=F32)


def _dot_nt(a, b):
    return lax.dot_general(a, b, (((1,), (1,)), ((), ())), preferred_element_type=F32)


def _dot_tn(a, b):
    return lax.dot_general(a, b, (((0,), (0,)), ((), ())), preferred_element_type=F32)


C_Q0, C_KV0, C_XB0, C_GB0, C_KR0, C_END = 0, 256, 384, 896, 1408, 1536


def _pre_kernel(x_ref, cos_ref, sin_ref, g_mix_ref, w_in_ref, g_q_ref, w_q_ref, w_uk_ref, g_kv_ref,
                q_ref, kc_ref, ckv_ref, kr_ref, xb_ref, gb_ref):
    x = x_ref[...]
    hn = _rms(x, g_mix_ref[...]).astype(BF16)
    proj = _dot(hn, w_in_ref[...])
    cos = cos_ref[...]
    sin = sin_ref[...]

    qn = _rms(proj[:, C_Q0:C_KV0], g_q_ref[...]).astype(BF16)
    qall = _dot(qn, w_q_ref[...])
    n_nope = N_HEADS * NOPE_DIM
    q_lat = _dot(qall[:, :n_nope].astype(BF16), w_uk_ref[...]) * MLA_SCALE
    lane = lax.broadcasted_iota(jnp.int32, (1, LANES), 1)
    for grp in range(N_HEADS // HEADS_PER_LANE_GROUP):
        a = n_nope + grp * LANES
        b = a + N_HEADS * ROPE_DIM
        rot = (qall[:, a:a + LANES] * cos + qall[:, b:b + LANES] * sin) * MLA_SCALE
        for j in range(HEADS_PER_LANE_GROUP):
            h = grp * HEADS_PER_LANE_GROUP + j
            own = (lane >= j * ROPE_DIM) & (lane < (j + 1) * ROPE_DIM)
            q_ref[h, :, :KV_RANK] = q_lat[:, h * KV_RANK:(h + 1) * KV_RANK].astype(BF16)
            q_ref[h, :, KV_RANK:] = jnp.where(own, rot, 0.0).astype(BF16)

    ckv = _rms(proj[:, C_KV0:C_XB0], g_kv_ref[...])
    ckv_ref[...] = ckv
    kx = proj[:, C_KR0:C_END]
    kr = kx * cos + pltpu.roll(kx, ROPE_HALF, axis=1) * sin
    kr_ref[...] = kr[:, :ROPE_DIM]
    kc_ref[:, :KV_RANK] = ckv.astype(BF16)
    kc_ref[:, KV_RANK:] = kr.astype(BF16)
    xb_ref[...] = proj[:, C_XB0:C_GB0]
    gb_ref[...] = proj[:, C_GB0:C_KR0]


def _pre_call(x, cos_t, sin_t, w):
    n, d = x.shape
    tm = min(TOKEN_TILE, n)
    n_pos_blocks = cos_t.shape[0] // tm
    row = lambda i: (i, 0)
    fixed = lambda i: (0, 0)
    pos = lambda i: (i % n_pos_blocks, 0)
    return pl.pallas_call(
        _pre_kernel,
        grid=(n // tm,),
        in_specs=[
            pl.BlockSpec((tm, d), row),
            pl.BlockSpec((tm, LANES), pos),
            pl.BlockSpec((tm, LANES), pos),
            pl.BlockSpec((1, d), fixed),
            pl.BlockSpec((d, C_END), fixed),
            pl.BlockSpec((1, Q_RANK), fixed),
            pl.BlockSpec(w['w_q'].shape, fixed),
            pl.BlockSpec(w['w_uk'].shape, fixed),
            pl.BlockSpec((1, KV_RANK), fixed),
        ],
        out_specs=[
            pl.BlockSpec((N_HEADS, tm, QK_WIDTH), lambda i: (0, i, 0)),
            pl.BlockSpec((tm, QK_WIDTH), row),
            pl.BlockSpec((tm, KV_RANK), row),
            pl.BlockSpec((tm, ROPE_DIM), row),
            pl.BlockSpec((tm, LRU_WIDTH), row),
            pl.BlockSpec((tm, LRU_WIDTH), row),
        ],
        out_shape=[
            jax.ShapeDtypeStruct((N_HEADS, n, QK_WIDTH), BF16),
            jax.ShapeDtypeStruct((n, QK_WIDTH), BF16),
            jax.ShapeDtypeStruct((n, KV_RANK), F32),
            jax.ShapeDtypeStruct((n, ROPE_DIM), F32),
            jax.ShapeDtypeStruct((n, LRU_WIDTH), F32),
            jax.ShapeDtypeStruct((n, LRU_WIDTH), F32),
        ],
        compiler_params=_params(("parallel",)),
        name="pre_mixer",
    )(x, cos_t, sin_t, w['g_mix'], w['w_in'], w['g_q'], w['w_q'], w['w_uk'], w['g_kv'])


def _mla_out_norm(acc_ref, l_ref, w_uv_ref, g_att_ref, rows):
    att = jnp.zeros((rows, N_HEADS * V_DIM), F32)
    for h in range(N_HEADS):
        sl = slice(h * rows, (h + 1) * rows)
        o_h = (acc_ref[sl, :] / l_ref[sl, :]).astype(BF16)
        att = att + _dot(o_h, w_uv_ref[h])
    return _rms(att, g_att_ref[...])


def _attn_prompt_kernel(q_ref, kc_ref, w_uv_ref, g_att_ref, o_ref, m_ref, l_ref, acc_ref):
    i = pl.program_id(1)
    tq, tk = ATTN_Q_TILE, ATTN_K_TILE
    rows = N_HEADS * tq
    q = q_ref[...].reshape(rows, QK_WIDTH)
    m_ref[...] = jnp.full((rows, 1), NEG_BIG, F32)
    l_ref[...] = jnp.zeros((rows, 1), F32)
    acc_ref[...] = jnp.zeros((rows, KV_RANK), F32)

    def step(j, masked):
        k = kc_ref[pl.ds(pl.multiple_of(j * tk, tk), tk), :]
        s = _dot_nt(q, k)
        if masked:
            q_pos = i * tq + (lax.broadcasted_iota(jnp.int32, (rows, 1), 0) & (tq - 1))
            k_pos = j * tk + lax.broadcasted_iota(jnp.int32, (1, tk), 1)
            s = jnp.where(k_pos <= q_pos, s, NEG_BIG)
        m_prev = m_ref[...]
        m_new = jnp.maximum(m_prev, jnp.max(s, axis=1, keepdims=True))
        alpha = jnp.exp(m_prev - m_new)
        p = jnp.exp(s - m_new)
        l_ref[...] = alpha * l_ref[...] + jnp.sum(p, axis=1, keepdims=True)
        acc_ref[...] = alpha * acc_ref[...] + _dot(p.astype(BF16), k[:, :KV_RANK])
        m_ref[...] = m_new

    n_full = (i * tq) // tk

    def body(j, carry):
        step(j, False)
        return carry

    lax.fori_loop(0, n_full, body, 0)
    step(n_full, True)
    o_ref[...] = _mla_out_norm(acc_ref, l_ref, w_uv_ref, g_att_ref, tq).astype(BF16)


def _attn_prompt_call(q, kc, w, batch, seq):
    tq = ATTN_Q_TILE
    nq = seq // tq
    rows = N_HEADS * tq
    return pl.pallas_call(
        _attn_prompt_kernel,
        grid=(batch, nq),
        in_specs=[
            pl.BlockSpec((N_HEADS, tq, QK_WIDTH), lambda b, i: (0, b * nq + i, 0)),
            pl.BlockSpec((seq, QK_WIDTH), lambda b, i: (b, 0)),
            pl.BlockSpec(w['w_uv'].shape, lambda b, i: (0, 0, 0)),
            pl.BlockSpec((1, N_HEADS * V_DIM), lambda b, i: (0, 0)),
        ],
        out_specs=pl.BlockSpec((tq, N_HEADS * V_DIM), lambda b, i: (b * nq + i, 0)),
        out_shape=jax.ShapeDtypeStruct((batch * seq, N_HEADS * V_DIM), BF16),
        scratch_shapes=[
            pltpu.VMEM((rows, 1), F32),
            pltpu.VMEM((rows, 1), F32),
            pltpu.VMEM((rows, KV_RANK), F32),
        ],
        compiler_params=_params(("parallel", "parallel")),
        name="mla_prompt",
    )(q, kc, w['w_uv'], w['g_att'])


def _attn_sample_kernel(pt_ref, q_ref, qr_ref, kc_ref, w_uv_ref, g_att_ref, ckv_hbm, kr_hbm,
                        o_ref, ckv_buf, kr_buf, sems, l_ref, acc_ref):
    b = pl.program_id(0)
    nb = pl.num_programs(0)
    n_pages = ckv_buf.shape[1]
    rows = N_HEADS * TOK_PAD

    def copies(item, slot, page):
        pid = pt_ref[item, page]
        return (pltpu.make_async_copy(ckv_hbm.at[pid], ckv_buf.at[slot, page], sems.at[0, slot]),
                pltpu.make_async_copy(kr_hbm.at[pid], kr_buf.at[slot, page], sems.at[1, slot]))

    def start_item(item, slot):
        def go(page, carry):
            c0, c1 = copies(item, slot, page)
            c0.start()
            c1.start()
            return carry
        lax.fori_loop(0, n_pages, go, 0)

    def wait_item(item, slot):
        def go(page, carry):
            c0, c1 = copies(item, slot, page)
            c0.wait()
            c1.wait()
            return carry
        lax.fori_loop(0, n_pages, go, 0)

    slot = b % 2

    @pl.when(b == 0)
    def _():
        start_item(0, 0)

    @pl.when(b + 1 < nb)
    def _():
        start_item(b + 1, 1 - slot)

    wait_item(b, slot)

    q = q_ref[0]
    q_lat = q[:, :KV_RANK]
    q_rope = qr_ref[0]
    n_keys = n_pages * PAGE_SIZE
    ckv = ckv_buf[slot].reshape(n_keys, KV_RANK).astype(BF16)
    kr = kr_buf[slot].reshape(n_keys, ROPE_DIM).astype(BF16)
    s_past = _dot_nt(q_lat, ckv) + _dot_nt(q_rope, kr)

    kc_new = kc_ref[0].astype(F32)
    n_new = kc_new.shape[0]
    qf = q.astype(F32)
    tok = lax.broadcasted_iota(jnp.int32, (rows, 1), 0) & (TOK_PAD - 1)
    s_new = []
    m = jnp.max(s_past, axis=1, keepdims=True)
    for t in range(n_new):
        s_t = jnp.sum(qf * kc_new[t:t + 1, :], axis=1, keepdims=True)
        s_t = jnp.where(tok >= t, s_t, NEG_BIG)
        s_new.append(s_t)
        m = jnp.maximum(m, s_t)
    p_past = jnp.exp(s_past - m)
    l = jnp.sum(p_past, axis=1, keepdims=True)
    acc = _dot(p_past.astype(BF16), ckv)
    for t in range(n_new):
        p_t = jnp.exp(s_new[t] - m)
        l = l + p_t
        acc = acc + p_t.astype(BF16).astype(F32) * kc_new[t:t + 1, :KV_RANK]
    l_ref[...] = l
    acc_ref[...] = acc
    o_ref[0] = _mla_out_norm(acc_ref, l_ref, w_uv_ref, g_att_ref, TOK_PAD)


def _attn_sample_call(page_table, q, q_rope, kc_new, cache_ckv, cache_krope, w):
    n_items, n_pages = page_table.shape
    rows = N_HEADS * TOK_PAD
    n_new = kc_new.shape[1]
    grid_spec = pltpu.PrefetchScalarGridSpec(
        num_scalar_prefetch=1,
        grid=(n_items,),
        in_specs=[
            pl.BlockSpec((1, rows, QK_WIDTH), lambda b, pt: (b, 0, 0)),
            pl.BlockSpec((1, rows, ROPE_DIM), lambda b, pt: (b, 0, 0)),
            pl.BlockSpec((1, n_new, QK_WIDTH), lambda b, pt: (b, 0, 0)),
            pl.BlockSpec(w['w_uv'].shape, lambda b, pt: (0, 0, 0)),
            pl.BlockSpec((1, N_HEADS * V_DIM), lambda b, pt: (0, 0)),
            pl.BlockSpec(memory_space=pl.ANY),
            pl.BlockSpec(memory_space=pl.ANY),
        ],
        out_specs=pl.BlockSpec((1, TOK_PAD, N_HEADS * V_DIM), lambda b, pt: (b, 0, 0)),
        scratch_shapes=[
            pltpu.VMEM((2, n_pages, PAGE_SIZE, KV_RANK), F32),
            pltpu.VMEM((2, n_pages, PAGE_SIZE, ROPE_DIM), F32),
            pltpu.SemaphoreType.DMA((2, 2)),
            pltpu.VMEM((rows, 1), F32),
            pltpu.VMEM((rows, KV_RANK), F32),
        ],
    )
    return pl.pallas_call(
        _attn_sample_kernel,
        grid_spec=grid_spec,
        out_shape=jax.ShapeDtypeStruct((n_items, TOK_PAD, N_HEADS * V_DIM), F32),
        compiler_params=_params(("arbitrary",)),
        name="mla_sample",
    )(page_table, q, q_rope, kc_new, w['w_uv'], w['g_att'], cache_ckv, cache_krope)


def _lru_gates(xc, gb, w):
    (wgx_ref, bgx_ref, wga_ref, bga_ref, lam_ref) = w
    xcb = xc.astype(BF16)
    gate_x = jax.nn.sigmoid(_dot(xcb, wgx_ref[...]) + bgx_ref[...])
    gate_a = jax.nn.sigmoid(_dot(xcb, wga_ref[...]) + bga_ref[...])
    neg_lam = -lam_ref[...]
    softplus = jnp.maximum(neg_lam, 0.0) + jnp.log1p(jnp.exp(-jnp.abs(neg_lam)))
    log_a = -LRU_C * gate_a * softplus
    a = jnp.exp(log_a)
    u = jnp.sqrt(1.0 - jnp.exp(2.0 * log_a)) * (gate_x * xc)
    return a, u, jax.nn.gelu(gb)


def _lru_prompt_kernel(xb_ref, gb_ref, cw_ref, cb_ref, wgx_ref, bgx_ref, wga_ref, bga_ref, lam_ref,
                       g_rec_ref, rec_ref, conv_ref, hlast_ref, xs_ref, h_ref):
    t = pl.program_id(1)
    tt = LRU_TIME_TILE
    halo = SUBLANES

    @pl.when(t == 0)
    def _():
        xs_ref[0:halo, :] = jnp.zeros((halo, LRU_WIDTH), F32)
        h_ref[...] = jnp.zeros((1, LRU_WIDTH), F32)

    xb = xb_ref[...]
    xs_ref[halo:, :] = xb
    xc = cb_ref[...] + xb * cw_ref[CONV_WIDTH - 1:CONV_WIDTH, :]
    for d in range(1, CONV_WIDTH):
        xc = xc + xs_ref[pl.ds(halo - d, tt), :] * cw_ref[CONV_WIDTH - 1 - d:CONV_WIDTH - d, :]
    xs_ref[0:halo, :] = xb[tt - halo:, :]

    a, u, gate = _lru_gates(xc, gb_ref[...], (wgx_ref, bgx_ref, wga_ref, bga_ref, lam_ref))

    row = lax.broadcasted_iota(jnp.int32, (tt, 1), 0)
    d = 1
    while d < tt:
        if d < SUBLANES:
            a_sh = pltpu.roll(a, d, axis=0)
            u_sh = pltpu.roll(u, d, axis=0)
            keep = row >= d
            u = jnp.where(keep, a * u_sh + u, u)
            a = jnp.where(keep, a * a_sh, a)
        else:
            a_sh = jnp.concatenate([jnp.ones((d, LRU_WIDTH), F32), a[:tt - d]], axis=0)
            u_sh = jnp.concatenate([jnp.zeros((d, LRU_WIDTH), F32), u[:tt - d]], axis=0)
            u = a * u_sh + u
            a = a * a_sh
        d *= 2
    h = a * h_ref[...] + u
    h_ref[...] = h[tt - 1:tt, :]
    rec_ref[...] = _rms(h * gate, g_rec_ref[...]).astype(BF16)
    conv_ref[0] = xb[tt - (CONV_WIDTH - 1):, :]
    hlast_ref[0] = h[tt - 1:tt, :]


def _lru_prompt_call(xb, gb, w, batch, seq):
    tt = LRU_TIME_TILE
    nt = seq // tt
    row = lambda b, t: (b * nt + t, 0)
    fixed = lambda b, t: (0, 0)
    wid = LRU_WIDTH
    return pl.pallas_call(
        _lru_prompt_kernel,
        grid=(batch, nt),
        in_specs=[
            pl.BlockSpec((tt, wid), row),
            pl.BlockSpec((tt, wid), row),
            pl.BlockSpec((CONV_WIDTH, wid), fixed),
            pl.BlockSpec((1, wid), fixed),
            pl.BlockSpec((wid, wid), fixed),
            pl.BlockSpec((1, wid), fixed),
            pl.BlockSpec((wid, wid), fixed),
            pl.BlockSpec((1, wid), fixed),
            pl.BlockSpec((1, wid), fixed),
            pl.BlockSpec((1, wid), fixed),
        ],
        out_specs=[
            pl.BlockSpec((tt, wid), row),
            pl.BlockSpec((1, CONV_WIDTH - 1, wid), lambda b, t: (b, 0, 0)),
            pl.BlockSpec((1, 1, wid), lambda b, t: (b, 0, 0)),
        ],
        out_shape=[
            jax.ShapeDtypeStruct((batch * seq, wid), BF16),
            jax.ShapeDtypeStruct((batch, CONV_WIDTH - 1, wid), F32),
            jax.ShapeDtypeStruct((batch, 1, wid), F32),
        ],
        scratch_shapes=[
            pltpu.VMEM((tt + SUBLANES, wid), F32),
            pltpu.VMEM((1, wid), F32),
        ],
        compiler_params=_params(("parallel", "arbitrary")),
        name="rglru_prompt",
    )(xb, gb, w['conv_w'], w['conv_b'], w['w_gx'], w['b_gx'], w['w_ga'], w['b_ga'], w['lam'], w['g_rec'])


def _lru_sample_kernel(xb_ref, gb_ref, cbuf_ref, h0_ref, cw_ref, cb_ref, wgx_ref, bgx_ref, wga_ref,
                       bga_ref, lam_ref, g_rec_ref, rec_ref, conv_ref, hlast_ref):
    n_t = xb_ref.shape[0]
    n_b = xb_ref.shape[1]
    xpad = [cbuf_ref[j] for j in range(CONV_WIDTH - 1)] + [xb_ref[t] for t in range(n_t)]
    xcs = []
    for t in range(n_t):
        xc = cb_ref[...]
        for j in range(CONV_WIDTH):
            xc = xc + xpad[t + j] * cw_ref[j:j + 1, :]
        xcs.append(xc)
    xc_all = jnp.concatenate(xcs, axis=0)
    gb_all = jnp.concatenate([gb_ref[t] for t in range(n_t)], axis=0)
    a, u, gate = _lru_gates(xc_all, gb_all, (wgx_ref, bgx_ref, wga_ref, bga_ref, lam_ref))
    h = h0_ref[...]
    for t in range(n_t):
        sl = slice(t * n_b, (t + 1) * n_b)
        h = a[sl] * h + u[sl]
        rec_ref[t] = _rms(h * gate[sl], g_rec_ref[...]).astype(BF16)
    for j in range(CONV_WIDTH - 1):
        conv_ref[j] = xpad[n_t + j]
    hlast_ref[...] = h


def _lru_sample_call(xb_t, gb_t, cbuf_t, h0, w):
    n_t, n_b, wid = xb_t.shape
    return pl.pallas_call(
        _lru_sample_kernel,
        out_shape=[
            jax.ShapeDtypeStruct((n_t, n_b, wid), BF16),
            jax.ShapeDtypeStruct((CONV_WIDTH - 1, n_b, wid), F32),
            jax.ShapeDtypeStruct((n_b, wid), F32),
        ],
        compiler_params=pltpu.CompilerParams(vmem_limit_bytes=VMEM_LIMIT),
        name="rglru_sample",
    )(xb_t, gb_t, cbuf_t, h0, w['conv_w'], w['conv_b'], w['w_gx'], w['b_gx'], w['w_ga'], w['b_ga'],
      w['lam'], w['g_rec'])


def _mix_out_kernel(x_ref, att_ref, rec_ref, w_oa_ref, w_or_ref, g_mem_ref, w_mq_ref, x1_ref, qm_ref):
    x1 = x_ref[...] + _dot(att_ref[...], w_oa_ref[...]) + _dot(rec_ref[...], w_or_ref[...])
    x1_ref[...] = x1
    hn = _rms(x1, g_mem_ref[...]).astype(BF16)
    qm_ref[...] = (_dot(hn, w_mq_ref[...]) * (MEM_HEAD_DIM ** -0.5)).astype(BF16)


def _mix_out_call(x, att, rec, w):
    n, d = x.shape
    tm = min(TOKEN_TILE, n)
    half = att.shape[1]
    row = lambda i: (i, 0)
    fixed = lambda i: (0, 0)
    return pl.pallas_call(
        _mix_out_kernel,
        grid=(n // tm,),
        in_specs=[
            pl.BlockSpec((tm, d), row),
            pl.BlockSpec((tm, half), row),
            pl.BlockSpec((tm, half), row),
            pl.BlockSpec((half, d), fixed),
            pl.BlockSpec((half, d), fixed),
            pl.BlockSpec((1, d), fixed),
            pl.BlockSpec((d, d), fixed),
        ],
        out_specs=[pl.BlockSpec((tm, d), row), pl.BlockSpec((tm, d), row)],
        out_shape=[jax.ShapeDtypeStruct((n, d), F32), jax.ShapeDtypeStruct((n, d), BF16)],
        compiler_params=_params(("parallel",)),
        name="mix_out",
    )(x, att, rec, w['w_o_att'], w['w_o_rec'], w['g_mem'], w['w_mq'])


def _mem_kv_kernel(mem_ref, g_ref, w_ref, k_ref, v_ref, kb_ref, vb_ref):
    d = mem_ref.shape[1]
    kv = _dot(_rms(mem_ref[...], g_ref[...]).astype(BF16), w_ref[...])
    k_ref[...] = kv[:, :d]
    v_ref[...] = kv[:, d:]
    kb_ref[...] = kv[:, :d].astype(BF16)
    vb_ref[...] = kv[:, d:].astype(BF16)


def _mem_kv_call(mem, w):
    n, d = mem.shape
    tm = min(TOKEN_TILE, n)
    row = lambda i: (i, 0)
    fixed = lambda i: (0, 0)
    return pl.pallas_call(
        _mem_kv_kernel,
        grid=(n // tm,),
        in_specs=[pl.BlockSpec((tm, d), row), pl.BlockSpec((1, d), fixed), pl.BlockSpec((d, 2 * d), fixed)],
        out_specs=[pl.BlockSpec((tm, d), row)] * 4,
        out_shape=[jax.ShapeDtypeStruct((n, d), F32)] * 2 + [jax.ShapeDtypeStruct((n, d), BF16)] * 2,
        compiler_params=_params(("parallel",)),
        name="mem_kv",
    )(mem, w['g_memtok'], w['w_mkv'])


def _mem_attn_prompt_kernel(q_ref, k_ref, v_ref, o_ref):
    for h in range(MEM_HEADS):
        sl = slice(h * MEM_HEAD_DIM, (h + 1) * MEM_HEAD_DIM)
        s = _dot_nt(q_ref[:, sl], k_ref[:, sl])
        p = jnp.exp(s - jnp.max(s, axis=1, keepdims=True))
        p = p / jnp.sum(p, axis=1, keepdims=True)
        o_ref[:, sl] = _dot(p.astype(BF16), v_ref[:, sl]).astype(BF16)


def _mem_attn_prompt_call(qm, kb, vb, batch, seq):
    n, d = qm.shape
    tm = min(TOKEN_TILE, n)
    nt = seq // tm
    n_mem = kb.shape[0] // batch
    return pl.pallas_call(
        _mem_attn_prompt_kernel,
        grid=(batch, nt),
        in_specs=[
            pl.BlockSpec((tm, d), lambda b, i: (b * nt + i, 0)),
            pl.BlockSpec((n_mem, d), lambda b, i: (b, 0)),
            pl.BlockSpec((n_mem, d), lambda b, i: (b, 0)),
        ],
        out_specs=pl.BlockSpec((tm, d), lambda b, i: (b * nt + i, 0)),
        out_shape=jax.ShapeDtypeStruct((n, d), BF16),
        compiler_params=_params(("parallel", "parallel")),
        name="mem_attn_prompt",
    )(qm, kb, vb)


def _mem_attn_sample_kernel(q_ref, k_ref, v_ref, o_ref):
    d = q_ref.shape[2]
    rows = MEM_HEADS * TOK_PAD
    lane_head = lax.shift_right_logical(lax.broadcasted_iota(jnp.int32, (rows, d), 1), MEM_HEAD_DIM.bit_length() - 1)
    row_head = lax.shift_right_logical(lax.broadcasted_iota(jnp.int32, (rows, d), 0), TOK_PAD.bit_length() - 1)
    own = lane_head == row_head
    for g in range(q_ref.shape[0]):
        q8 = q_ref[g]
        wt = jnp.where(own, jnp.concatenate([q8] * MEM_HEADS, axis=0), 0.0).astype(BF16)
        kb = k_ref[g].astype(BF16)
        s = _dot_nt(kb, wt)
        p = jnp.exp(s - jnp.max(s, axis=0, keepdims=True))
        p = p / jnp.sum(p, axis=0, keepdims=True)
        o_wide = _dot_tn(p.astype(BF16), v_ref[g].astype(BF16))
        o_wide = jnp.where(own, o_wide, 0.0)
        o = o_wide[0:TOK_PAD]
        for h in range(1, MEM_HEADS):
            o = o + o_wide[h * TOK_PAD:(h + 1) * TOK_PAD]
        o_ref[g] = o


def _mem_attn_sample_call(q8, k, v):
    n_items, _, d = q8.shape
    n_mem = k.shape[1]
    g = MEM_ITEMS
    blk = lambda i: (i, 0, 0)
    return pl.pallas_call(
        _mem_attn_sample_kernel,
        grid=(n_items // g,),
        in_specs=[
            pl.BlockSpec((g, TOK_PAD, d), blk),
            pl.BlockSpec((g, n_mem, d), blk),
            pl.BlockSpec((g, n_mem, d), blk),
        ],
        out_specs=pl.BlockSpec((g, TOK_PAD, d), blk),
        out_shape=jax.ShapeDtypeStruct((n_items, TOK_PAD, d), F32),
        compiler_params=_params(("parallel",)),
        name="mem_attn_sample",
    )(q8, k, v)


def _router_kernel(x1_ref, o_ref, w_mo_ref, g_moe_ref, w_r_ref, b_r_ref, x2_ref, hn_ref, idx_ref, gate_ref):
    x2 = x1_ref[...] + _dot(o_ref[...], w_mo_ref[...])
    x2_ref[...] = x2
    hn = _rms(x2, g_moe_ref[...]).astype(BF16)
    hn_ref[...] = hn
    vals = _dot(hn, w_r_ref[...]) + b_r_ref[...]
    lane = lax.broadcasted_iota(jnp.int32, vals.shape, 1)
    lane_f = lane.astype(F32)
    idx_out = jnp.zeros(vals.shape, F32)
    top = []
    for k in range(TOP_K):
        m = jnp.max(vals, axis=1, keepdims=True)
        idx = jnp.min(jnp.where(vals == m, lane_f, float(LANES)), axis=1, keepdims=True)
        idx_out = jnp.where(lane == k, idx, idx_out)
        vals = jnp.where(lane_f == idx, -jnp.inf, vals)
        top.append(m)
    e = [jnp.exp(t - top[0]) for t in top]
    denom = e[0] + e[1] + e[2] + e[3]
    gate_out = jnp.zeros(vals.shape, F32)
    for k in range(TOP_K):
        gate_out = jnp.where(lane == k, e[k] / denom, gate_out)
    idx_ref[...] = idx_out.astype(jnp.int32)
    gate_ref[...] = gate_out


def _router_call(x1, o, w):
    n, d = x1.shape
    tm = min(TOKEN_TILE, n)
    row = lambda i: (i, 0)
    fixed = lambda i: (0, 0)
    return pl.pallas_call(
        _router_kernel,
        grid=(n // tm,),
        in_specs=[
            pl.BlockSpec((tm, d), row),
            pl.BlockSpec((tm, d), row),
            pl.BlockSpec((d, d), fixed),
            pl.BlockSpec((1, d), fixed),
            pl.BlockSpec((d, LANES), fixed),
            pl.BlockSpec((1, LANES), fixed),
        ],
        out_specs=[pl.BlockSpec((tm, d), row), pl.BlockSpec((tm, d), row),
                   pl.BlockSpec((tm, LANES), row), pl.BlockSpec((tm, LANES), row)],
        out_shape=[jax.ShapeDtypeStruct((n, d), F32), jax.ShapeDtypeStruct((n, d), BF16),
                   jax.ShapeDtypeStruct((n, LANES), jnp.int32), jax.ShapeDtypeStruct((n, LANES), F32)],
        compiler_params=_params(("parallel",)),
        name="router",
    )(x1, o, w['w_mo'], w['g_moe'], w['w_router'], w['b_router'])


def _expert_kernel(be_ref, nb_ref, x_ref, wgu_ref, bgu_ref, wdn_ref, bdn_ref, o_ref):
    i = pl.program_id(0)

    @pl.when(i < nb_ref[0])
    def _():
        gu = _dot(x_ref[...], wgu_ref[0]) + bgu_ref[0]
        gate = jnp.minimum(gu[:, :EXPERT_FF], SWIGLU_LIMIT)
        lin = jnp.clip(gu[:, EXPERT_FF:], -SWIGLU_LIMIT, SWIGLU_LIMIT)
        act = ((lin + 1.0) * gate * jax.nn.sigmoid(SWIGLU_ALPHA * gate)).astype(BF16)
        o_ref[...] = _dot(act, wdn_ref[0]) + bdn_ref[0]

    @pl.when(i >= nb_ref[0])
    def _():
        o_ref[...] = jnp.zeros(o_ref.shape, F32)


def _expert_call(block_expert, n_used, xs, w):
    n_rows, d = xs.shape
    tm = MOE_TILE
    ff2 = 2 * EXPERT_FF
    grid_spec = pltpu.PrefetchScalarGridSpec(
        num_scalar_prefetch=2,
        grid=(n_rows // tm,),
        in_specs=[
            pl.BlockSpec((tm, d), lambda i, be, nb: (i, 0)),
            pl.BlockSpec((1, d, ff2), lambda i, be, nb: (be[i], 0, 0)),
            pl.BlockSpec((1, 1, ff2), lambda i, be, nb: (be[i], 0, 0)),
            pl.BlockSpec((1, EXPERT_FF, d), lambda i, be, nb: (be[i], 0, 0)),
            pl.BlockSpec((1, 1, d), lambda i, be, nb: (be[i], 0, 0)),
        ],
        out_specs=pl.BlockSpec((tm, d), lambda i, be, nb: (i, 0)),
    )
    return pl.pallas_call(
        _expert_kernel,
        grid_spec=grid_spec,
        out_shape=jax.ShapeDtypeStruct((n_rows, d), F32),
        compiler_params=_params(("arbitrary",)),
        name="experts",
    )(block_expert, n_used, xs, w['w_gu'], w['b_gu'], w['w_dn'], w['b_dn'])


def _combine_kernel(x2_ref, y_ref, gate_ref, g_ref, o_ref):
    gates = gate_ref[...]
    y = y_ref[:, 0, :] * gates[:, 0:1]
    for k in range(1, TOP_K):
        y = y + y_ref[:, k, :] * gates[:, k:k + 1]
    o_ref[...] = _rms(x2_ref[...] + y, g_ref[...])


def _combine_call(x2, y_rows, gates, g_final):
    n, d = x2.shape
    tm = min(TOKEN_TILE, n)
    row = lambda i: (i, 0)
    return pl.pallas_call(
        _combine_kernel,
        grid=(n // tm,),
        in_specs=[
            pl.BlockSpec((tm, d), row),
            pl.BlockSpec((tm, TOP_K, d), lambda i: (i, 0, 0)),
            pl.BlockSpec((tm, LANES), row),
            pl.BlockSpec((1, d), lambda i: (0, 0)),
        ],
        out_specs=pl.BlockSpec((tm, d), row),
        out_shape=jax.ShapeDtypeStruct((n, d), F32),
        compiler_params=_params(("parallel",)),
        name="combine",
    )(x2, y_rows, gates, g_final)


def _moe(x2, hn, idx128, gates128, w, g_final):
    n, d = x2.shape
    tm = MOE_TILE
    nk = n * TOP_K
    e_flat = idx128[:, :TOP_K].reshape(-1)
    order = jnp.argsort(e_flat)
    se = e_flat[order]
    stok = order // TOP_K
    counts = jnp.bincount(e_flat, length=N_EXPERTS)
    starts = jnp.cumsum(counts) - counts
    padded = ((counts + tm - 1) // tm) * tm
    pends = jnp.cumsum(padded)
    pstarts = pends - padded
    dest = pstarts[se] + (jnp.arange(nk) - starts[se])
    n_blocks = -(-nk // tm) + N_EXPERTS
    row_tok = jnp.zeros((n_blocks * tm,), jnp.int32).at[dest].set(stok.astype(jnp.int32))
    block_e = jnp.clip(jnp.searchsorted(pends, jnp.arange(n_blocks) * tm, side='right'), 0, N_EXPERTS - 1)
    n_used = (pends[-1] // tm).astype(jnp.int32).reshape(1)
    xs = hn[row_tok]
    out_rows = _expert_call(block_e.astype(jnp.int32), n_used, xs, w)
    slot = jnp.zeros((nk,), jnp.int32).at[order].set(dest.astype(jnp.int32))
    y_rows = out_rows[slot].reshape(n, TOP_K, d)
    return _combine_call(x2, y_rows, gates128, g_final)


def _block_diag(blocks):
    nb, r, c = blocks.shape
    eye = jnp.eye(nb, dtype=blocks.dtype)
    return jnp.einsum('nrc,nm->nrmc', blocks, eye).reshape(nb * r, nb * c)


def _prep_weights(lp):
    w = {}
    d = lp['w_in'].shape[0]
    c0 = Q_RANK
    c1 = c0 + KV_RANK
    c2 = c1 + ROPE_DIM
    c3 = c2 + LRU_WIDTH
    w_in = lp['w_in']
    kr_cols = jnp.tile(w_in[:, c1:c2], (1, HEADS_PER_LANE_GROUP))
    w['w_in'] = jnp.concatenate([w_in[:, :c1], w_in[:, c2:c3], w_in[:, c3:], kr_cols], axis=1).astype(BF16)
    w['g_mix'] = lp['norm_mix'].reshape(1, d)
    w['g_q'] = lp['norm_q_lat'].reshape(1, Q_RANK)
    w['g_kv'] = lp['norm_kv_lat'].reshape(1, KV_RANK)
    wq = lp['w_q_b'].reshape(Q_RANK, N_HEADS, NOPE_DIM + ROPE_DIM)
    nope = wq[..., :NOPE_DIM].reshape(Q_RANK, N_HEADS * NOPE_DIM)
    x1 = wq[..., NOPE_DIM:NOPE_DIM + ROPE_HALF]
    x2 = wq[..., NOPE_DIM + ROPE_HALF:]
    rope_a = jnp.concatenate([x1, x2], axis=-1).reshape(Q_RANK, N_HEADS * ROPE_DIM)
    rope_b = jnp.concatenate([x2, x1], axis=-1).reshape(Q_RANK, N_HEADS * ROPE_DIM)
    w['w_q'] = jnp.concatenate([nope, rope_a, rope_b], axis=1).astype(BF16)
    wkv = lp['w_kv_b'].reshape(KV_RANK, N_HEADS, NOPE_DIM + V_DIM)
    w_uk = jnp.transpose(wkv[..., :NOPE_DIM], (1, 2, 0))
    w['w_uk'] = _block_diag(w_uk).astype(BF16)
    w_uv = jnp.transpose(wkv[..., NOPE_DIM:], (1, 0, 2))
    eye = jnp.eye(N_HEADS, dtype=F32)
    w['w_uv'] = jnp.einsum('hrv,hg->hrgv', w_uv, eye).reshape(N_HEADS, KV_RANK, N_HEADS * V_DIM).astype(BF16)
    w['g_att'] = lp['norm_att_out'].reshape(1, -1)
    w['conv_w'] = lp['conv_w']
    w['conv_b'] = lp['conv_b'].reshape(1, LRU_WIDTH)
    w['w_gx'] = _block_diag(lp['w_gate_x']).astype(BF16)
    w['b_gx'] = lp['b_gate_x'].reshape(1, LRU_WIDTH)
    w['w_ga'] = _block_diag(lp['w_gate_a']).astype(BF16)
    w['b_ga'] = lp['b_gate_a'].reshape(1, LRU_WIDTH)
    w['lam'] = lp['lru_lambda'].reshape(1, LRU_WIDTH)
    w['g_rec'] = lp['norm_rec_out'].reshape(1, LRU_WIDTH)
    n_att = N_HEADS * V_DIM
    w['w_o_att'] = lp['w_o'][:n_att].astype(BF16)
    w['w_o_rec'] = lp['w_o'][n_att:].astype(BF16)
    w['g_mem'] = lp['norm_mem_attn'].reshape(1, d)
    w['g_memtok'] = lp['norm_memtok'].reshape(1, d)
    w['w_mq'] = lp['w_mem_q'].astype(BF16)
    w['w_mkv'] = lp['w_mem_kv'].astype(BF16)
    w['w_mo'] = lp['w_mem_o'].astype(BF16)
    w['g_moe'] = lp['norm_moe'].reshape(1, d)
    w['w_router'] = jnp.pad(lp['w_router'], ((0, 0), (0, LANES - N_EXPERTS))).astype(BF16)
    w['b_router'] = jnp.pad(lp['b_router'], (0, LANES - N_EXPERTS), constant_values=NEG_BIG).reshape(1, LANES)
    w['w_gu'] = lp['w_gate_up'].astype(BF16)
    w['b_gu'] = lp['b_gate_up'].reshape(N_EXPERTS, 1, 2 * EXPERT_FF)
    w['w_dn'] = lp['w_down'].astype(BF16)
    w['b_dn'] = lp['b_down'].reshape(N_EXPERTS, 1, d)
    return w


def _rope_tables(pos, rows):
    inv = ROPE_THETA ** (-jnp.arange(ROPE_HALF, dtype=F32) / ROPE_HALF)
    ang = pos.astype(F32)[:, None] * inv[None, :]
    cos, sin = jnp.cos(ang), jnp.sin(ang)
    cos_t = jnp.tile(jnp.concatenate([cos, cos], axis=1), (1, HEADS_PER_LANE_GROUP))
    sin_t = jnp.tile(jnp.concatenate([-sin, sin], axis=1), (1, HEADS_PER_LANE_GROUP))
    reps = rows // pos.shape[0]
    return jnp.tile(cos_t, (reps, 1)), jnp.tile(sin_t, (reps, 1))


def _layer_prompt(x, mem, w, g_final):
    batch, seq, d = x.shape
    n = batch * seq
    x2d = x.reshape(n, d)
    cos_t, sin_t = _rope_tables(jnp.arange(seq), seq)
    q, kc, ckv, kr, xb, gb = _pre_call(x2d, cos_t, sin_t, w)
    att = _attn_prompt_call(q, kc, w, batch, seq)
    rec, conv_buf, h_last = _lru_prompt_call(xb, gb, w, batch, seq)
    x1, qm = _mix_out_call(x2d, att, rec, w)
    n_mem = mem.shape[1]
    mk, mv, mkb, mvb = _mem_kv_call(mem.reshape(batch * n_mem, d), w)
    o = _mem_attn_prompt_call(qm, mkb, mvb, batch, seq)
    x2, hn, idx, gates = _router_call(x1, o, w)
    y = _moe(x2, hn, idx, gates, w, g_final)
    return (y.reshape(batch, seq, d), ckv.reshape(batch, seq, KV_RANK), kr.reshape(batch, seq, ROPE_DIM),
            conv_buf, h_last.reshape(batch, LRU_WIDTH),
            mk.reshape(batch, n_mem, MEM_HEADS, MEM_HEAD_DIM), mv.reshape(batch, n_mem, MEM_HEADS, MEM_HEAD_DIM))


def _layer_sample(x, page_table, pool_ckv, pool_krope, conv_buf, h0, mk, mv, w, g_final):
    batch, n_t, d = x.shape
    n = batch * n_t
    n_past = page_table.shape[1] * PAGE_SIZE
    x2d = x.reshape(n, d)
    cos_t, sin_t = _rope_tables(n_past + jnp.arange(n_t), min(TOKEN_TILE, n))
    q, kc, ckv, kr, xb, gb = _pre_call(x2d, cos_t, sin_t, w)
    q_item = jnp.transpose(q.reshape(N_HEADS, batch, n_t, QK_WIDTH), (1, 0, 2, 3))
    q_item = jnp.pad(q_item, ((0, 0), (0, 0), (0, TOK_PAD - n_t), (0, 0)))
    rope_parts = [q_item[:, h, :, KV_RANK + (h % HEADS_PER_LANE_GROUP) * ROPE_DIM:
                         KV_RANK + (h % HEADS_PER_LANE_GROUP + 1) * ROPE_DIM] for h in range(N_HEADS)]
    q_rope = jnp.stack(rope_parts, axis=1).reshape(batch, N_HEADS * TOK_PAD, ROPE_DIM)
    q_item = q_item.reshape(batch, N_HEADS * TOK_PAD, QK_WIDTH)
    att8 = _attn_sample_call(page_table, q_item, q_rope, kc.reshape(batch, n_t, QK_WIDTH), pool_ckv, pool_krope, w)
    att = att8[:, :n_t].reshape(n, N_HEADS * V_DIM).astype(BF16)
    tmaj = lambda a: jnp.swapaxes(a, 0, 1)
    rec_t, conv_t, h_last = _lru_sample_call(tmaj(xb.reshape(batch, n_t, LRU_WIDTH)), tmaj(gb.reshape(batch, n_t, LRU_WIDTH)),
                                             tmaj(conv_buf), h0, w)
    rec = tmaj(rec_t).reshape(n, LRU_WIDTH)
    x1, qm = _mix_out_call(x2d, att, rec, w)
    q8 = jnp.pad(qm.reshape(batch, n_t, d).astype(F32), ((0, 0), (0, TOK_PAD - n_t), (0, 0)))
    n_mem = mk.shape[1]
    o8 = _mem_attn_sample_call(q8, mk.reshape(batch, n_mem, d), mv.reshape(batch, n_mem, d))
    o = o8[:, :n_t].reshape(n, d).astype(BF16)
    x2, hn, idx, gates = _router_call(x1, o, w)
    y = _moe(x2, hn, idx, gates, w, g_final)
    return (y.reshape(batch, n_t, d), ckv.reshape(batch, n_t, KV_RANK), kr.reshape(batch, n_t, ROPE_DIM),
            tmaj(conv_t), h_last)


def kernel(x_prompt, x_sample, mem_prompt, cache_ckv, cache_krope, cache_conv, state_rglru, cache_mem_k, cache_mem_v, page_table, w_in, norm_mix, norm_q_lat, w_q_b, norm_kv_lat, w_kv_b, conv_w, conv_b, w_gate_x, b_gate_x, w_gate_a, b_gate_a, lru_lambda, norm_att_out, norm_rec_out, w_o, norm_mem_attn, norm_memtok, w_mem_q, w_mem_kv, w_mem_o, norm_moe, w_router, b_router, w_gate_up, b_gate_up, w_down, b_down, norm_final):
    depth = w_in.shape[0]
    assert depth == 1, "the final norm is fused into the last layer's combine kernel"
    g_final = norm_final.reshape(1, -1)
    lp = dict(w_in=w_in[0], norm_mix=norm_mix[0], norm_q_lat=norm_q_lat[0], w_q_b=w_q_b[0],
              norm_kv_lat=norm_kv_lat[0], w_kv_b=w_kv_b[0], conv_w=conv_w[0], conv_b=conv_b[0],
              w_gate_x=w_gate_x[0], b_gate_x=b_gate_x[0], w_gate_a=w_gate_a[0], b_gate_a=b_gate_a[0],
              lru_lambda=lru_lambda[0], norm_att_out=norm_att_out[0], norm_rec_out=norm_rec_out[0],
              w_o=w_o[0], norm_mem_attn=norm_mem_attn[0], norm_memtok=norm_memtok[0],
              w_mem_q=w_mem_q[0], w_mem_kv=w_mem_kv[0], w_mem_o=w_mem_o[0], norm_moe=norm_moe[0],
              w_router=w_router[0], b_router=b_router[0], w_gate_up=w_gate_up[0],
              b_gate_up=b_gate_up[0], w_down=w_down[0], b_down=b_down[0])
    w = _prep_weights(lp)
    yp, ckv_p, kr_p, conv_p, h_p, mk_p, mv_p = _layer_prompt(x_prompt, mem_prompt, w, g_final)
    ys, ckv_s, kr_s, conv_s, h_s = _layer_sample(x_sample, page_table, cache_ckv[0], cache_krope[0],
                                                 cache_conv[0], state_rglru[0], cache_mem_k[0],
                                                 cache_mem_v[0], w, g_final)
    return (yp, ys, ckv_p[None], kr_p[None], conv_p[None], h_p[None], mk_p[None], mv_p[None],
            ckv_s[None], kr_s[None], conv_s[None], h_s[None])
```

```python
import functools

import jax
import jax.numpy as jnp
from jax import lax
from jax.experimental import pallas as pl
from jax.experimental.pallas import tpu as pltpu

F32 = jnp.float32
BF16 = jnp.bfloat16

N_HEADS = 8
NOPE_DIM = 64
V_DIM = 64
ROPE_DIM = 32
ROPE_HALF = ROPE_DIM // 2
KV_RANK = 128
Q_RANK = 256
LRU_WIDTH = 512
LRU_BLOCKS = 8
CONV_WIDTH = 4
LRU_C = 8.0
MEM_HEADS = 4
MEM_HEAD_DIM = 256
N_EXPERTS = 32
TOP_K = 4
EXPERT_FF = 1024
SWIGLU_LIMIT = 7.0
SWIGLU_ALPHA = 1.702
ROPE_THETA = 10000.0
EPS = 1e-6
MLA_SCALE = (NOPE_DIM + ROPE_DIM) ** -0.5
PAGE_SIZE = 128

LANES = 128
SUBLANES = 8
HEADS_PER_LANE_GROUP = LANES // ROPE_DIM
QK_WIDTH = KV_RANK + LANES
NEG_BIG = -1e30
VMEM_LIMIT = 56 * 1024 * 1024

TOKEN_TILE = 256
ATTN_Q_TILE = 128
ATTN_K_TILE = 512
LRU_TIME_TILE = 256
MOE_TILE = 256
TOK_PAD = SUBLANES


def _params(sem, vmem=VMEM_LIMIT):
    return pltpu.CompilerParams(dimension_semantics=sem, vmem_limit_bytes=vmem)


def _rms(x, g):
    return x * lax.rsqrt(jnp.mean(x * x, axis=-1, keepdims=True) + EPS) * g


def _dot(a, b):
    return jnp.dot(a, b, preferred_element_type=F32)


def _dot_nt(a, b):
    return lax.dot_general(a, b, (((1,), (1,)), ((), ())), preferred_element_type=F32)


C_Q0, C_KV0, C_XB0, C_GB0, C_KR0, C_END = 0, 256, 384, 896, 1408, 1536


def _pre_kernel(x_ref, cos_ref, sin_ref, g_mix_ref, w_in_ref, g_q_ref, w_q_ref, w_uk_ref, g_kv_ref,
                q_ref, kc_ref, ckv_ref, kr_ref, xb_ref, gb_ref):
    x = x_ref[...]
    hn = _rms(x, g_mix_ref[...]).astype(BF16)
    proj = _dot(hn, w_in_ref[...])
    cos = cos_ref[...]
    sin = sin_ref[...]

    qn = _rms(proj[:, C_Q0:C_KV0], g_q_ref[...]).astype(BF16)
    qall = _dot(qn, w_q_ref[...])
    n_nope = N_HEADS * NOPE_DIM
    q_lat = _dot(qall[:, :n_nope].astype(BF16), w_uk_ref[...]) * MLA_SCALE
    lane = lax.broadcasted_iota(jnp.int32, (1, LANES), 1)
    for grp in range(N_HEADS // HEADS_PER_LANE_GROUP):
        a = n_nope + grp * LANES
        b = a + N_HEADS * ROPE_DIM
        rot = (qall[:, a:a + LANES] * cos + qall[:, b:b + LANES] * sin) * MLA_SCALE
        for j in range(HEADS_PER_LANE_GROUP):
            h = grp * HEADS_PER_LANE_GROUP + j
            own = (lane >= j * ROPE_DIM) & (lane < (j + 1) * ROPE_DIM)
            q_ref[h, :, :KV_RANK] = q_lat[:, h * KV_RANK:(h + 1) * KV_RANK].astype(BF16)
            q_ref[h, :, KV_RANK:] = jnp.where(own, rot, 0.0).astype(BF16)

    ckv = _rms(proj[:, C_KV0:C_XB0], g_kv_ref[...])
    ckv_ref[...] = ckv
    kx = proj[:, C_KR0:C_END]
    kr = kx * cos + pltpu.roll(kx, ROPE_HALF, axis=1) * sin
    kr_ref[...] = kr[:, :ROPE_DIM]
    kc_ref[:, :KV_RANK] = ckv.astype(BF16)
    kc_ref[:, KV_RANK:] = kr.astype(BF16)
    xb_ref[...] = proj[:, C_XB0:C_GB0]
    gb_ref[...] = proj[:, C_GB0:C_KR0]


def _pre_call(x, cos_t, sin_t, w):
    n, d = x.shape
    tm = min(TOKEN_TILE, n)
    n_pos_blocks = cos_t.shape[0] // tm
    row = lambda i: (i, 0)
    fixed = lambda i: (0, 0)
    pos = lambda i: (i % n_pos_blocks, 0)
    return pl.pallas_call(
        _pre_kernel,
        grid=(n // tm,),
        in_specs=[
            pl.BlockSpec((tm, d), row),
            pl.BlockSpec((tm, LANES), pos),
            pl.BlockSpec((tm, LANES), pos),
            pl.BlockSpec((1, d), fixed),
            pl.BlockSpec((d, C_END), fixed),
            pl.BlockSpec((1, Q_RANK), fixed),
            pl.BlockSpec(w['w_q'].shape, fixed),
            pl.BlockSpec(w['w_uk'].shape, fixed),
            pl.BlockSpec((1, KV_RANK), fixed),
        ],
        out_specs=[
            pl.BlockSpec((N_HEADS, tm, QK_WIDTH), lambda i: (0, i, 0)),
            pl.BlockSpec((tm, QK_WIDTH), row),
            pl.BlockSpec((tm, KV_RANK), row),
            pl.BlockSpec((tm, ROPE_DIM), row),
            pl.BlockSpec((tm, LRU_WIDTH), row),
            pl.BlockSpec((tm, LRU_WIDTH), row),
        ],
        out_shape=[
            jax.ShapeDtypeStruct((N_HEADS, n, QK_WIDTH), BF16),
            jax.ShapeDtypeStruct((n, QK_WIDTH), BF16),
            jax.ShapeDtypeStruct((n, KV_RANK), F32),
            jax.ShapeDtypeStruct((n, ROPE_DIM), F32),
            jax.ShapeDtypeStruct((n, LRU_WIDTH), F32),
            jax.ShapeDtypeStruct((n, LRU_WIDTH), F32),
        ],
        compiler_params=_params(("parallel",)),
        name="pre_mixer",
    )(x, cos_t, sin_t, w['g_mix'], w['w_in'], w['g_q'], w['w_q'], w['w_uk'], w['g_kv'])


def _mla_out_norm(acc_ref, l_ref, w_uv_ref, g_att_ref, rows):
    att = jnp.zeros((rows, N_HEADS * V_DIM), F32)
    for h in range(N_HEADS):
        sl = slice(h * rows, (h + 1) * rows)
        o_h = (acc_ref[sl, :] / l_ref[sl, :]).astype(BF16)
        att = att + _dot(o_h, w_uv_ref[h])
    return _rms(att, g_att_ref[...])


def _attn_prompt_kernel(q_ref, kc_ref, w_uv_ref, g_att_ref, o_ref, m_ref, l_ref, acc_ref):
    i = pl.program_id(1)
    tq, tk = ATTN_Q_TILE, ATTN_K_TILE
    rows = N_HEADS * tq
    q = q_ref[...].reshape(rows, QK_WIDTH)
    m_ref[...] = jnp.full((rows, 1), NEG_BIG, F32)
    l_ref[...] = jnp.zeros((rows, 1), F32)
    acc_ref[...] = jnp.zeros((rows, KV_RANK), F32)

    def step(j, masked):
        k = kc_ref[pl.ds(pl.multiple_of(j * tk, tk), tk), :]
        s = _dot_nt(q, k)
        if masked:
            q_pos = i * tq + (lax.broadcasted_iota(jnp.int32, (rows, 1), 0) & (tq - 1))
            k_pos = j * tk + lax.broadcasted_iota(jnp.int32, (1, tk), 1)
            s = jnp.where(k_pos <= q_pos, s, NEG_BIG)
        m_prev = m_ref[...]
        m_new = jnp.maximum(m_prev, jnp.max(s, axis=1, keepdims=True))
        alpha = jnp.exp(m_prev - m_new)
        p = jnp.exp(s - m_new)
        l_ref[...] = alpha * l_ref[...] + jnp.sum(p, axis=1, keepdims=True)
        acc_ref[...] = alpha * acc_ref[...] + _dot(p.astype(BF16), k[:, :KV_RANK])
        m_ref[...] = m_new

    n_full = (i * tq) // tk

    def body(j, carry):
        step(j, False)
        return carry

    lax.fori_loop(0, n_full, body, 0)
    step(n_full, True)
    o_ref[...] = _mla_out_norm(acc_ref, l_ref, w_uv_ref, g_att_ref, tq).astype(BF16)


def _attn_prompt_call(q, kc, w, batch, seq):
    tq = ATTN_Q_TILE
    nq = seq // tq
    rows = N_HEADS * tq
    return pl.pallas_call(
        _attn_prompt_kernel,
        grid=(batch, nq),
        in_specs=[
            pl.BlockSpec((N_HEADS, tq, QK_WIDTH), lambda b, i: (0, b * nq + i, 0)),
            pl.BlockSpec((seq, QK_WIDTH), lambda b, i: (b, 0)),
            pl.BlockSpec(w['w_uv'].shape, lambda b, i: (0, 0, 0)),
            pl.BlockSpec((1, N_HEADS * V_DIM), lambda b, i: (0, 0)),
        ],
        out_specs=pl.BlockSpec((tq, N_HEADS * V_DIM), lambda b, i: (b * nq + i, 0)),
        out_shape=jax.ShapeDtypeStruct((batch * seq, N_HEADS * V_DIM), BF16),
        scratch_shapes=[
            pltpu.VMEM((rows, 1), F32),
            pltpu.VMEM((rows, 1), F32),
            pltpu.VMEM((rows, KV_RANK), F32),
        ],
        compiler_params=_params(("parallel", "parallel")),
        name="mla_prompt",
    )(q, kc, w['w_uv'], w['g_att'])


def _attn_sample_kernel(pt_ref, q_ref, qr_ref, kc_ref, w_uv_ref, g_att_ref, ckv_hbm, kr_hbm,
                        o_ref, ckv_buf, kr_buf, sems, l_ref, acc_ref):
    b = pl.program_id(0)
    nb = pl.num_programs(0)
    n_pages = ckv_buf.shape[1]
    rows = N_HEADS * TOK_PAD

    def copies(item, slot, page):
        pid = pt_ref[item, page]
        lanes = pl.ds(pl.multiple_of(page * PAGE_SIZE, PAGE_SIZE), PAGE_SIZE)
        return (pltpu.make_async_copy(ckv_hbm.at[pid], ckv_buf.at[slot, page], sems.at[0, slot]),
                pltpu.make_async_copy(kr_hbm.at[pid], kr_buf.at[slot, :, lanes], sems.at[1, slot]))

    def start_item(item, slot):
        def go(page, carry):
            c0, c1 = copies(item, slot, page)
            c0.start()
            c1.start()
            return carry
        lax.fori_loop(0, n_pages, go, 0)

    def wait_item(item, slot):
        def go(page, carry):
            c0, c1 = copies(item, slot, page)
            c0.wait()
            c1.wait()
            return carry
        lax.fori_loop(0, n_pages, go, 0)

    slot = b % 2

    @pl.when(b == 0)
    def _():
        start_item(0, 0)

    @pl.when(b + 1 < nb)
    def _():
        start_item(b + 1, 1 - slot)

    wait_item(b, slot)

    q = q_ref[0]
    q_lat = q[:, :KV_RANK]
    q_rope = qr_ref[0]
    n_keys = n_pages * PAGE_SIZE
    ckv = ckv_buf[slot].reshape(n_keys, KV_RANK).astype(BF16)
    kr_t = kr_buf[slot].astype(BF16)
    s_past = _dot_nt(q_lat, ckv) + _dot(q_rope, kr_t)

    kc_new = kc_ref[0].astype(F32)
    n_new = kc_new.shape[0]
    qf = q.astype(F32)
    tok = lax.broadcasted_iota(jnp.int32, (rows, 1), 0) & (TOK_PAD - 1)
    s_new = []
    m = jnp.max(s_past, axis=1, keepdims=True)
    for t in range(n_new):
        s_t = jnp.sum(qf * kc_new[t:t + 1, :], axis=1, keepdims=True)
        s_t = jnp.where(tok >= t, s_t, NEG_BIG)
        s_new.append(s_t)
        m = jnp.maximum(m, s_t)
    p_past = jnp.exp(s_past - m)
    l = jnp.sum(p_past, axis=1, keepdims=True)
    acc = _dot(p_past.astype(BF16), ckv)
    for t in range(n_new):
        p_t = jnp.exp(s_new[t] - m)
        l = l + p_t
        acc = acc + p_t.astype(BF16).astype(F32) * kc_new[t:t + 1, :KV_RANK]
    l_ref[...] = l
    acc_ref[...] = acc
    o_ref[0] = _mla_out_norm(acc_ref, l_ref, w_uv_ref, g_att_ref, TOK_PAD)


def _attn_sample_call(page_table, q, q_rope, kc_new, cache_ckv, cache_krope, w):
    n_items, n_pages = page_table.shape
    rows = N_HEADS * TOK_PAD
    n_new = kc_new.shape[1]
    grid_spec = pltpu.PrefetchScalarGridSpec(
        num_scalar_prefetch=1,
        grid=(n_items,),
        in_specs=[
            pl.BlockSpec((1, rows, QK_WIDTH), lambda b, pt: (b, 0, 0)),
            pl.BlockSpec((1, rows, ROPE_DIM), lambda b, pt: (b, 0, 0)),
            pl.BlockSpec((1, n_new, QK_WIDTH), lambda b, pt: (b, 0, 0)),
            pl.BlockSpec(w['w_uv'].shape, lambda b, pt: (0, 0, 0)),
            pl.BlockSpec((1, N_HEADS * V_DIM), lambda b, pt: (0, 0)),
            pl.BlockSpec(memory_space=pl.ANY),
            pl.BlockSpec(memory_space=pl.ANY),
        ],
        out_specs=pl.BlockSpec((1, TOK_PAD, N_HEADS * V_DIM), lambda b, pt: (b, 0, 0)),
        scratch_shapes=[
            pltpu.VMEM((2, n_pages, PAGE_SIZE, KV_RANK), F32),
            pltpu.VMEM((2, ROPE_DIM, n_pages * PAGE_SIZE), F32),
            pltpu.SemaphoreType.DMA((2, 2)),
            pltpu.VMEM((rows, 1), F32),
            pltpu.VMEM((rows, KV_RANK), F32),
        ],
    )
    return pl.pallas_call(
        _attn_sample_kernel,
        grid_spec=grid_spec,
        out_shape=jax.ShapeDtypeStruct((n_items, TOK_PAD, N_HEADS * V_DIM), F32),
        compiler_params=_params(("arbitrary",)),
        name="mla_sample",
    )(page_table, q, q_rope, kc_new, w['w_uv'], w['g_att'], cache_ckv, cache_krope)


def _lru_gates(xc, gb, w):
    (wgx_ref, bgx_ref, wga_ref, bga_ref, lam_ref) = w
    xcb = xc.astype(BF16)
    gate_x = jax.nn.sigmoid(_dot(xcb, wgx_ref[...]) + bgx_ref[...])
    gate_a = jax.nn.sigmoid(_dot(xcb, wga_ref[...]) + bga_ref[...])
    neg_lam = -lam_ref[...]
    softplus = jnp.maximum(neg_lam, 0.0) + jnp.log1p(jnp.exp(-jnp.abs(neg_lam)))
    log_a = -LRU_C * gate_a * softplus
    a = jnp.exp(log_a)
    u = jnp.sqrt(1.0 - jnp.exp(2.0 * log_a)) * (gate_x * xc)
    return a, u, jax.nn.gelu(gb)


def _lru_prompt_kernel(xb_ref, gb_ref, cw_ref, cb_ref, wgx_ref, bgx_ref, wga_ref, bga_ref, lam_ref,
                       g_rec_ref, rec_ref, conv_ref, hlast_ref, xs_ref, h_ref):
    t = pl.program_id(1)
    tt = LRU_TIME_TILE
    halo = SUBLANES

    @pl.when(t == 0)
    def _():
        xs_ref[0:halo, :] = jnp.zeros((halo, LRU_WIDTH), F32)
        h_ref[...] = jnp.zeros((1, LRU_WIDTH), F32)

    xb = xb_ref[...]
    xs_ref[halo:, :] = xb
    xc = cb_ref[...] + xb * cw_ref[CONV_WIDTH - 1:CONV_WIDTH, :]
    for d in range(1, CONV_WIDTH):
        xc = xc + xs_ref[pl.ds(halo - d, tt), :] * cw_ref[CONV_WIDTH - 1 - d:CONV_WIDTH - d, :]
    xs_ref[0:halo, :] = xb[tt - halo:, :]

    a, u, gate = _lru_gates(xc, gb_ref[...], (wgx_ref, bgx_ref, wga_ref, bga_ref, lam_ref))

    row = lax.broadcasted_iota(jnp.int32, (tt, 1), 0)
    d = 1
    while d < tt:
        if d < SUBLANES:
            a_sh = pltpu.roll(a, d, axis=0)
            u_sh = pltpu.roll(u, d, axis=0)
            keep = row >= d
            u = jnp.where(keep, a * u_sh + u, u)
            a = jnp.where(keep, a * a_sh, a)
        else:
            a_sh = jnp.concatenate([jnp.ones((d, LRU_WIDTH), F32), a[:tt - d]], axis=0)
            u_sh = jnp.concatenate([jnp.zeros((d, LRU_WIDTH), F32), u[:tt - d]], axis=0)
            u = a * u_sh + u
            a = a * a_sh
        d *= 2
    h = a * h_ref[...] + u
    h_ref[...] = h[tt - 1:tt, :]
    rec_ref[...] = _rms(h * gate, g_rec_ref[...]).astype(BF16)
    conv_ref[0] = xb[tt - (CONV_WIDTH - 1):, :]
    hlast_ref[0] = h[tt - 1:tt, :]


def _lru_prompt_call(xb, gb, w, batch, seq):
    tt = LRU_TIME_TILE
    nt = seq // tt
    row = lambda b, t: (b * nt + t, 0)
    fixed = lambda b, t: (0, 0)
    wid = LRU_WIDTH
    return pl.pallas_call(
        _lru_prompt_kernel,
        grid=(batch, nt),
        in_specs=[
            pl.BlockSpec((tt, wid), row),
            pl.BlockSpec((tt, wid), row),
            pl.BlockSpec((CONV_WIDTH, wid), fixed),
            pl.BlockSpec((1, wid), fixed),
            pl.BlockSpec((wid, wid), fixed),
            pl.BlockSpec((1, wid), fixed),
            pl.BlockSpec((wid, wid), fixed),
            pl.BlockSpec((1, wid), fixed),
            pl.BlockSpec((1, wid), fixed),
            pl.BlockSpec((1, wid), fixed),
        ],
        out_specs=[
            pl.BlockSpec((tt, wid), row),
            pl.BlockSpec((1, CONV_WIDTH - 1, wid), lambda b, t: (b, 0, 0)),
            pl.BlockSpec((1, 1, wid), lambda b, t: (b, 0, 0)),
        ],
        out_shape=[
            jax.ShapeDtypeStruct((batch * seq, wid), BF16),
            jax.ShapeDtypeStruct((batch, CONV_WIDTH - 1, wid), F32),
            jax.ShapeDtypeStruct((batch, 1, wid), F32),
        ],
        scratch_shapes=[
            pltpu.VMEM((tt + SUBLANES, wid), F32),
            pltpu.VMEM((1, wid), F32),
        ],
        compiler_params=_params(("parallel", "arbitrary")),
        name="rglru_prompt",
    )(xb, gb, w['conv_w'], w['conv_b'], w['w_gx'], w['b_gx'], w['w_ga'], w['b_ga'], w['lam'], w['g_rec'])


def _lru_sample_kernel(xb_ref, gb_ref, cbuf_ref, h0_ref, cw_ref, cb_ref, wgx_ref, bgx_ref, wga_ref,
                       bga_ref, lam_ref, g_rec_ref, rec_ref, conv_ref, hlast_ref):
    n_t = xb_ref.shape[0]
    n_b = xb_ref.shape[1]
    xpad = [cbuf_ref[j] for j in range(CONV_WIDTH - 1)] + [xb_ref[t] for t in range(n_t)]
    xcs = []
    for t in range(n_t):
        xc = cb_ref[...]
        for j in range(CONV_WIDTH):
            xc = xc + xpad[t + j] * cw_ref[j:j + 1, :]
        xcs.append(xc)
    xc_all = jnp.concatenate(xcs, axis=0)
    gb_all = jnp.concatenate([gb_ref[t] for t in range(n_t)], axis=0)
    a, u, gate = _lru_gates(xc_all, gb_all, (wgx_ref, bgx_ref, wga_ref, bga_ref, lam_ref))
    h = h0_ref[...]
    for t in range(n_t):
        sl = slice(t * n_b, (t + 1) * n_b)
        h = a[sl] * h + u[sl]
        rec_ref[t] = _rms(h * gate[sl], g_rec_ref[...]).astype(BF16)
    for j in range(CONV_WIDTH - 1):
        conv_ref[j] = xpad[n_t + j]
    hlast_ref[...] = h


def _lru_sample_call(xb_t, gb_t, cbuf_t, h0, w):
    n_t, n_b, wid = xb_t.shape
    return pl.pallas_call(
        _lru_sample_kernel,
        out_shape=[
            jax.ShapeDtypeStruct((n_t, n_b, wid), BF16),
            jax.ShapeDtypeStruct((CONV_WIDTH - 1, n_b, wid), F32),
            jax.ShapeDtypeStruct((n_b, wid), F32),
        ],
        compiler_params=pltpu.CompilerParams(vmem_limit_bytes=VMEM_LIMIT),
        name="rglru_sample",
    )(xb_t, gb_t, cbuf_t, h0, w['conv_w'], w['conv_b'], w['w_gx'], w['b_gx'], w['w_ga'], w['b_ga'],
      w['lam'], w['g_rec'])


def _mix_out_kernel(x_ref, att_ref, rec_ref, w_oa_ref, w_or_ref, g_mem_ref, w_mq_ref, x1_ref, qm_ref):
    x1 = x_ref[...] + _dot(att_ref[...], w_oa_ref[...]) + _dot(rec_ref[...], w_or_ref[...])
    x1_ref[...] = x1
    hn = _rms(x1, g_mem_ref[...]).astype(BF16)
    qm_ref[...] = (_dot(hn, w_mq_ref[...]) * (MEM_HEAD_DIM ** -0.5)).astype(BF16)


def _mix_out_call(x, att, rec, w):
    n, d = x.shape
    tm = min(TOKEN_TILE, n)
    half = att.shape[1]
    row = lambda i: (i, 0)
    fixed = lambda i: (0, 0)
    return pl.pallas_call(
        _mix_out_kernel,
        grid=(n // tm,),
        in_specs=[
            pl.BlockSpec((tm, d), row),
            pl.BlockSpec((tm, half), row),
            pl.BlockSpec((tm, half), row),
            pl.BlockSpec((half, d), fixed),
            pl.BlockSpec((half, d), fixed),
            pl.BlockSpec((1, d), fixed),
            pl.BlockSpec((d, d), fixed),
        ],
        out_specs=[pl.BlockSpec((tm, d), row), pl.BlockSpec((tm, d), row)],
        out_shape=[jax.ShapeDtypeStruct((n, d), F32), jax.ShapeDtypeStruct((n, d), BF16)],
        compiler_params=_params(("parallel",)),
        name="mix_out",
    )(x, att, rec, w['w_o_att'], w['w_o_rec'], w['g_mem'], w['w_mq'])


def _mem_kv_kernel(mem_ref, g_ref, w_ref, k_ref, v_ref, kb_ref, vb_ref):
    d = mem_ref.shape[1]
    kv = _dot(_rms(mem_ref[...], g_ref[...]).astype(BF16), w_ref[...])
    k_ref[...] = kv[:, :d]
    v_ref[...] = kv[:, d:]
    kb_ref[...] = kv[:, :d].astype(BF16)
    vb_ref[...] = kv[:, d:].astype(BF16)


def _mem_kv_call(mem, w):
    n, d = mem.shape
    tm = min(TOKEN_TILE, n)
    row = lambda i: (i, 0)
    fixed = lambda i: (0, 0)
    return pl.pallas_call(
        _mem_kv_kernel,
        grid=(n // tm,),
        in_specs=[pl.BlockSpec((tm, d), row), pl.BlockSpec((1, d), fixed), pl.BlockSpec((d, 2 * d), fixed)],
        out_specs=[pl.BlockSpec((tm, d), row)] * 4,
        out_shape=[jax.ShapeDtypeStruct((n, d), F32)] * 2 + [jax.ShapeDtypeStruct((n, d), BF16)] * 2,
        compiler_params=_params(("parallel",)),
        name="mem_kv",
    )(mem, w['g_memtok'], w['w_mkv'])


def _mem_attn_prompt_kernel(q_ref, k_ref, v_ref, o_ref):
    for h in range(MEM_HEADS):
        sl = slice(h * MEM_HEAD_DIM, (h + 1) * MEM_HEAD_DIM)
        s = _dot_nt(q_ref[:, sl], k_ref[:, sl])
        p = jnp.exp(s - jnp.max(s, axis=1, keepdims=True))
        p = p / jnp.sum(p, axis=1, keepdims=True)
        o_ref[:, sl] = _dot(p.astype(BF16), v_ref[:, sl]).astype(BF16)


def _mem_attn_prompt_call(qm, kb, vb, batch, seq):
    n, d = qm.shape
    tm = min(TOKEN_TILE, n)
    nt = seq // tm
    n_mem = kb.shape[0] // batch
    return pl.pallas_call(
        _mem_attn_prompt_kernel,
        grid=(batch, nt),
        in_specs=[
            pl.BlockSpec((tm, d), lambda b, i: (b * nt + i, 0)),
            pl.BlockSpec((n_mem, d), lambda b, i: (b, 0)),
            pl.BlockSpec((n_mem, d), lambda b, i: (b, 0)),
        ],
        out_specs=pl.BlockSpec((tm, d), lambda b, i: (b * nt + i, 0)),
        out_shape=jax.ShapeDtypeStruct((n, d), BF16),
        compiler_params=_params(("parallel", "parallel")),
        name="mem_attn_prompt",
    )(qm, kb, vb)


def _mem_attn_sample_kernel(q_ref, k_hbm, v_hbm, o_ref, kbuf, vbuf, sems):
    b = pl.program_id(0)
    nb = pl.num_programs(0)

    def copies(item, slot):
        cps = []
        for h in range(MEM_HEADS):
            cps.append(pltpu.make_async_copy(k_hbm.at[item, :, h, :], kbuf.at[slot, h], sems.at[0, slot]))
            cps.append(pltpu.make_async_copy(v_hbm.at[item, :, h, :], vbuf.at[slot, h], sems.at[1, slot]))
        return cps

    slot = b % 2

    @pl.when(b == 0)
    def _():
        for cp in copies(0, 0):
            cp.start()

    @pl.when(b + 1 < nb)
    def _():
        for cp in copies(b + 1, 1 - slot):
            cp.start()

    for cp in copies(b, slot):
        cp.wait()

    q = q_ref[0]
    for h in range(MEM_HEADS):
        sl = slice(h * MEM_HEAD_DIM, (h + 1) * MEM_HEAD_DIM)
        s = _dot_nt(q[:, sl].astype(BF16), kbuf[slot, h].astype(BF16))
        p = jnp.exp(s - jnp.max(s, axis=1, keepdims=True))
        p = p / jnp.sum(p, axis=1, keepdims=True)
        o_ref[0, :, sl] = _dot(p.astype(BF16), vbuf[slot, h].astype(BF16))


def _mem_attn_sample_call(q8, k, v):
    n_items, _, d = q8.shape
    n_mem = k.shape[1]
    blk = lambda i: (i, 0, 0)
    return pl.pallas_call(
        _mem_attn_sample_kernel,
        grid=(n_items,),
        in_specs=[
            pl.BlockSpec((1, TOK_PAD, d), blk),
            pl.BlockSpec(memory_space=pl.ANY),
            pl.BlockSpec(memory_space=pl.ANY),
        ],
        out_specs=pl.BlockSpec((1, TOK_PAD, d), blk),
        out_shape=jax.ShapeDtypeStruct((n_items, TOK_PAD, d), F32),
        scratch_shapes=[
            pltpu.VMEM((2, MEM_HEADS, n_mem, MEM_HEAD_DIM), F32),
            pltpu.VMEM((2, MEM_HEADS, n_mem, MEM_HEAD_DIM), F32),
            pltpu.SemaphoreType.DMA((2, 2)),
        ],
        compiler_params=_params(("arbitrary",)),
        name="mem_attn_sample",
    )(q8, k, v)


def _router_kernel(x1_ref, o_ref, w_mo_ref, g_moe_ref, w_r_ref, b_r_ref, cnt0_ref,
                   x2_ref, hn_ref, idx_ref, gate_ref, rank_ref, cnt_ref, run_ref):
    @pl.when(pl.program_id(0) == 0)
    def _():
        run_ref[...] = cnt0_ref[...]

    x2 = x1_ref[...] + _dot(o_ref[...], w_mo_ref[...])
    x2_ref[...] = x2
    hn = _rms(x2, g_moe_ref[...]).astype(BF16)
    half = hn.shape[1] // 2
    hn_bits = pltpu.bitcast(hn.astype(F32), jnp.uint32)
    hn_ref[...] = (hn_bits[:, half:] & jnp.uint32(0xFFFF0000)) | (hn_bits[:, :half] >> 16)

    vals = _dot(hn, w_r_ref[...]) + b_r_ref[...]
    tm = vals.shape[0]
    lane = lax.broadcasted_iota(jnp.int32, vals.shape, 1)
    lane_f = lane.astype(F32)
    idx_out = jnp.zeros(vals.shape, F32)
    top, hot = [], []
    for k in range(TOP_K):
        m = jnp.max(vals, axis=1, keepdims=True)
        idx = jnp.min(jnp.where(vals == m, lane_f, float(LANES)), axis=1, keepdims=True)
        idx_out = jnp.where(lane == k, idx, idx_out)
        sel = lane_f == idx
        vals = jnp.where(sel, -jnp.inf, vals)
        top.append(m)
        hot.append(sel)
    e = [jnp.exp(t - top[0]) for t in top]
    denom = e[0] + e[1] + e[2] + e[3]
    gate_out = jnp.zeros(vals.shape, F32)
    for k in range(TOP_K):
        gate_out = jnp.where(lane == k, e[k] / denom, gate_out)
    idx_ref[...] = idx_out.astype(jnp.int32)
    gate_ref[...] = gate_out

    picked = jnp.where(hot[0] | hot[1] | hot[2] | hot[3], 1.0, 0.0)
    earlier = (lax.broadcasted_iota(jnp.int32, (tm, tm), 0) > lax.broadcasted_iota(jnp.int32, (tm, tm), 1))
    before = run_ref[...] + _dot(jnp.where(earlier, 1.0, 0.0).astype(BF16), picked.astype(BF16))
    rank_out = jnp.zeros(vals.shape, F32)
    for k in range(TOP_K):
        rank_k = jnp.sum(jnp.where(hot[k], before, 0.0), axis=1, keepdims=True)
        rank_out = jnp.where(lane == k, rank_k, rank_out)
    rank_ref[...] = rank_out.astype(jnp.int32)
    run_ref[...] = run_ref[...] + jnp.sum(picked, axis=0, keepdims=True)
    cnt_ref[...] = run_ref[...]


def _router_call(x1, o, count0, w):
    n, d = x1.shape
    tm = min(TOKEN_TILE, n)
    row = lambda i: (i, 0)
    fixed = lambda i: (0, 0)
    return pl.pallas_call(
        _router_kernel,
        grid=(n // tm,),
        in_specs=[
            pl.BlockSpec((tm, d), row),
            pl.BlockSpec((tm, d), row),
            pl.BlockSpec((d, d), fixed),
            pl.BlockSpec((1, d), fixed),
            pl.BlockSpec((d, LANES), fixed),
            pl.BlockSpec((1, LANES), fixed),
            pl.BlockSpec((1, LANES), fixed),
        ],
        out_specs=[pl.BlockSpec((tm, d), row), pl.BlockSpec((tm, d // 2), row),
                   pl.BlockSpec((tm, LANES), row), pl.BlockSpec((tm, LANES), row),
                   pl.BlockSpec((tm, LANES), row), pl.BlockSpec((1, LANES), fixed)],
        out_shape=[jax.ShapeDtypeStruct((n, d), F32), jax.ShapeDtypeStruct((n, d // 2), jnp.uint32),
                   jax.ShapeDtypeStruct((n, LANES), jnp.int32), jax.ShapeDtypeStruct((n, LANES), F32),
                   jax.ShapeDtypeStruct((n, LANES), jnp.int32), jax.ShapeDtypeStruct((1, LANES), F32)],
        scratch_shapes=[pltpu.VMEM((1, LANES), F32)],
        compiler_params=_params(("arbitrary",)),
        name="router",
    )(x1, o, w['w_mo'], w['g_moe'], w['w_router'], w['b_router'], count0)


def _row_copies(dest_ref, n_tok, make, start):
    def body(t, carry):
        for k in range(TOP_K):
            cp = make(t, k, dest_ref[0, 0, t * TOP_K + k])
            if start:
                cp.start()
            else:
                cp.wait()
        return carry
    lax.fori_loop(0, n_tok, body, 0, unroll=4)


def _dispatch_kernel(n_prompt_tiles, pend_ref, dest_ref, hp_ref, hs_ref, xs_ref, zero_ref, sem, zsem):
    i = pl.program_id(0)
    n_tok = hp_ref.shape[0]
    tm = zero_ref.shape[0]

    @pl.when(i == 0)
    def _():
        zero_ref[...] = jnp.zeros(zero_ref.shape, zero_ref.dtype)
        for start in (True, False):
            for e in range(N_EXPERTS):
                end = pend_ref[e]
                begin = pend_ref[e - 1] if e else 0

                @pl.when(end > begin)
                def _():
                    cp = pltpu.make_async_copy(zero_ref, xs_ref.at[pl.ds(pl.multiple_of(end - tm, tm), tm)], zsem)
                    if start:
                        cp.start()
                    else:
                        cp.wait()

        def fill(blk, carry):
            cp = pltpu.make_async_copy(zero_ref, xs_ref.at[pl.ds(pl.multiple_of(blk * tm, tm), tm)], zsem)
            cp.start()
            cp.wait()
            return carry
        lax.fori_loop(pend_ref[N_EXPERTS - 1] // tm, xs_ref.shape[0] // tm, fill, 0)

    def scatter(src_ref):
        make = lambda t, k, row: pltpu.make_async_copy(src_ref.at[pl.ds(t, 1)], xs_ref.at[pl.ds(row, 1)], sem)
        _row_copies(dest_ref, n_tok, make, True)
        _row_copies(dest_ref, n_tok, make, False)

    @pl.when(i < n_prompt_tiles)
    def _():
        scatter(hp_ref)

    @pl.when(i >= n_prompt_tiles)
    def _():
        scatter(hs_ref)


def _dispatch_call(pends, dest, hn_p, hn_s, n_rows):
    n_tiles, _, per_tile = dest.shape
    td = per_tile // TOP_K
    half = hn_p.shape[1]
    n_prompt_tiles = hn_p.shape[0] // td
    n_sample_tiles = hn_s.shape[0] // td
    kernel_fn = functools.partial(_dispatch_kernel, n_prompt_tiles)
    grid_spec = pltpu.PrefetchScalarGridSpec(
        num_scalar_prefetch=1,
        grid=(n_tiles,),
        in_specs=[
            pl.BlockSpec((1, 1, per_tile), lambda i, pe: (i, 0, 0), memory_space=pltpu.SMEM),
            pl.BlockSpec((td, half), lambda i, pe: (jnp.minimum(i, n_prompt_tiles - 1), 0)),
            pl.BlockSpec((td, half), lambda i, pe: (jnp.clip(i - n_prompt_tiles, 0, n_sample_tiles - 1), 0)),
        ],
        out_specs=pl.BlockSpec(memory_space=pl.ANY),
        scratch_shapes=[pltpu.VMEM((MOE_TILE, half), jnp.uint32), pltpu.SemaphoreType.DMA(()),
                        pltpu.SemaphoreType.DMA(())],
    )
    return pl.pallas_call(
        kernel_fn,
        grid_spec=grid_spec,
        out_shape=jax.ShapeDtypeStruct((n_rows, half), jnp.uint32),
        compiler_params=_params(("arbitrary",)),
        name="dispatch",
    )(pends, dest, hn_p, hn_s)


def _expert_kernel(be_ref, nb_ref, x_ref, wgu_ref, bgu_ref, wdn_ref, bdn_ref, o_ref, wgu_bf, wdn_bf):
    i = pl.program_id(0)
    used = i < nb_ref[0]
    new_expert = (i == 0) | (be_ref[i] != be_ref[jnp.maximum(i - 1, 0)])
    chunk = 128

    @pl.when(used & new_expert)
    def _():
        def cast(c, carry):
            rows = pl.ds(pl.multiple_of(c * chunk, chunk), chunk)
            wgu_bf[rows, :] = wgu_ref[0, rows, :].astype(BF16)
            wdn_bf[rows, :] = wdn_ref[0, rows, :].astype(BF16)
            return carry
        lax.fori_loop(0, wgu_bf.shape[0] // chunk, cast, 0)

    @pl.when(used)
    def _():
        half = x_ref.shape[1]
        bits = x_ref[...]
        x_lo = pltpu.bitcast(bits << 16, F32).astype(BF16)
        x_hi = pltpu.bitcast(bits & jnp.uint32(0xFFFF0000), F32).astype(BF16)
        gu = _dot(x_lo, wgu_bf[:half, :]) + _dot(x_hi, wgu_bf[half:, :]) + bgu_ref[0]
        gate = jnp.minimum(gu[:, :EXPERT_FF], SWIGLU_LIMIT)
        lin = jnp.clip(gu[:, EXPERT_FF:], -SWIGLU_LIMIT, SWIGLU_LIMIT)
        act = ((lin + 1.0) * gate * jax.nn.sigmoid(SWIGLU_ALPHA * gate)).astype(BF16)
        o_ref[...] = _dot(act, wdn_bf[...]) + bdn_ref[0]

    @pl.when(jnp.logical_not(used))
    def _():
        o_ref[...] = jnp.zeros(o_ref.shape, F32)


def _expert_call(block_expert, n_used, xs, w):
    n_rows, half = xs.shape
    d = 2 * half
    tm = MOE_TILE
    ff2 = 2 * EXPERT_FF
    grid_spec = pltpu.PrefetchScalarGridSpec(
        num_scalar_prefetch=2,
        grid=(n_rows // tm,),
        in_specs=[
            pl.BlockSpec((tm, half), lambda i, be, nb: (jnp.minimum(i, nb[0] - 1), 0)),
            pl.BlockSpec((1, d, ff2), lambda i, be, nb: (be[i], 0, 0)),
            pl.BlockSpec((1, 1, ff2), lambda i, be, nb: (be[i], 0, 0)),
            pl.BlockSpec((1, EXPERT_FF, d), lambda i, be, nb: (be[i], 0, 0)),
            pl.BlockSpec((1, 1, d), lambda i, be, nb: (be[i], 0, 0)),
        ],
        out_specs=pl.BlockSpec((tm, d), lambda i, be, nb: (i, 0)),
        scratch_shapes=[pltpu.VMEM((d, ff2), BF16), pltpu.VMEM((EXPERT_FF, d), BF16)],
    )
    return pl.pallas_call(
        _expert_kernel,
        grid_spec=grid_spec,
        out_shape=jax.ShapeDtypeStruct((n_rows, d), F32),
        compiler_params=_params(("arbitrary",)),
        name="experts",
    )(block_expert, n_used, xs, w['w_gu'], w['b_gu'], w['w_dn'], w['b_dn'])


def _combine_kernel(n, dcur_ref, dnext_ref, x2_ref, gate_ref, g_ref, rows_hbm, o_ref, gbuf, sems):
    i = pl.program_id(0)
    n_tok = x2_ref.shape[0]
    slot = i % 2

    def gather(dref, slot, start):
        make = lambda t, k, row: pltpu.make_async_copy(rows_hbm.at[pl.ds(row, 1)],
                                                       gbuf.at[slot, k, pl.ds(t, 1)], sems.at[slot])
        _row_copies(dref, n_tok, make, start)

    @pl.when(i == 0)
    def _():
        gather(dcur_ref, 0, True)

    @pl.when(i + 1 < n)
    def _():
        gather(dnext_ref, 1 - slot, True)

    gather(dcur_ref, slot, False)
    gates = gate_ref[...]
    y = gbuf[slot, 0] * gates[:, 0:1]
    for k in range(1, TOP_K):
        y = y + gbuf[slot, k] * gates[:, k:k + 1]
    o_ref[...] = _rms(x2_ref[...] + y, g_ref[...])


def _combine_call(dest, tile0, x2, gates, g_final, out_rows):
    n, d = x2.shape
    per_tile = dest.shape[2]
    td = per_tile // TOP_K
    n_tiles = n // td
    row = lambda i: (i, 0)
    return pl.pallas_call(
        functools.partial(_combine_kernel, n_tiles),
        grid=(n_tiles,),
        in_specs=[
            pl.BlockSpec((1, 1, per_tile), lambda i: (tile0 + i, 0, 0), memory_space=pltpu.SMEM),
            pl.BlockSpec((1, 1, per_tile), lambda i: (tile0 + jnp.minimum(i + 1, n_tiles - 1), 0, 0),
                         memory_space=pltpu.SMEM),
            pl.BlockSpec((td, d), row),
            pl.BlockSpec((td, LANES), row),
            pl.BlockSpec((1, d), lambda i: (0, 0)),
            pl.BlockSpec(memory_space=pl.ANY),
        ],
        out_specs=pl.BlockSpec((td, d), row),
        out_shape=jax.ShapeDtypeStruct((n, d), F32),
        scratch_shapes=[pltpu.VMEM((2, TOP_K, td, d), F32), pltpu.SemaphoreType.DMA((2,))],
        compiler_params=_params(("arbitrary",)),
        name="combine",
    )(dest, dest, x2, gates, g_final, out_rows)


def _moe(routed_p, routed_s, counts, w, g_final):
    tm = MOE_TILE
    n_p, n_s = routed_p[0].shape[0], routed_s[0].shape[0]
    td = min(TOKEN_TILE, n_p, n_s)
    nk = (n_p + n_s) * TOP_K
    counts = counts[0, :N_EXPERTS].astype(jnp.int32)
    padded = ((counts + tm - 1) // tm) * tm
    pends = jnp.cumsum(padded)
    pstarts = pends - padded
    idx = jnp.concatenate([routed_p[2][:, :TOP_K], routed_s[2][:, :TOP_K]], axis=0)
    rank = jnp.concatenate([routed_p[4][:, :TOP_K], routed_s[4][:, :TOP_K]], axis=0)
    hot = idx[:, :, None] == jnp.arange(N_EXPERTS, dtype=jnp.int32)[None, None, :]
    dest = rank + jnp.sum(jnp.where(hot, pstarts[None, None, :], 0), axis=-1)
    dest = dest.astype(jnp.int32).reshape((n_p + n_s) // td, 1, td * TOP_K)
    n_blocks = -(-nk // tm) + N_EXPERTS
    block_start = jnp.arange(n_blocks, dtype=jnp.int32) * tm
    block_e = jnp.minimum(jnp.sum(block_start[:, None] >= pends[None, :], axis=1), N_EXPERTS - 1)
    n_used = (pends[-1] // tm).astype(jnp.int32).reshape(1)
    xs = _dispatch_call(pends.astype(jnp.int32), dest, routed_p[1], routed_s[1], n_blocks * tm)
    out_rows = _expert_call(block_e.astype(jnp.int32), n_used, xs, w)
    y_p = _combine_call(dest, 0, routed_p[0], routed_p[3], g_final, out_rows)
    y_s = _combine_call(dest, n_p // td, routed_s[0], routed_s[3], g_final, out_rows)
    return y_p, y_s


def _block_diag(blocks):
    nb, r, c = blocks.shape
    eye = jnp.eye(nb, dtype=blocks.dtype)
    return jnp.einsum('nrc,nm->nrmc', blocks, eye).reshape(nb * r, nb * c)


def _prep_weights(lp):
    w = {}
    d = lp['w_in'].shape[0]
    c0 = Q_RANK
    c1 = c0 + KV_RANK
    c2 = c1 + ROPE_DIM
    c3 = c2 + LRU_WIDTH
    w_in = lp['w_in']
    kr_cols = jnp.tile(w_in[:, c1:c2], (1, HEADS_PER_LANE_GROUP))
    w['w_in'] = jnp.concatenate([w_in[:, :c1], w_in[:, c2:c3], w_in[:, c3:], kr_cols], axis=1).astype(BF16)
    w['g_mix'] = lp['norm_mix'].reshape(1, d)
    w['g_q'] = lp['norm_q_lat'].reshape(1, Q_RANK)
    w['g_kv'] = lp['norm_kv_lat'].reshape(1, KV_RANK)
    wq = lp['w_q_b'].reshape(Q_RANK, N_HEADS, NOPE_DIM + ROPE_DIM)
    nope = wq[..., :NOPE_DIM].reshape(Q_RANK, N_HEADS * NOPE_DIM)
    x1 = wq[..., NOPE_DIM:NOPE_DIM + ROPE_HALF]
    x2 = wq[..., NOPE_DIM + ROPE_HALF:]
    rope_a = jnp.concatenate([x1, x2], axis=-1).reshape(Q_RANK, N_HEADS * ROPE_DIM)
    rope_b = jnp.concatenate([x2, x1], axis=-1).reshape(Q_RANK, N_HEADS * ROPE_DIM)
    w['w_q'] = jnp.concatenate([nope, rope_a, rope_b], axis=1).astype(BF16)
    wkv = lp['w_kv_b'].reshape(KV_RANK, N_HEADS, NOPE_DIM + V_DIM)
    w_uk = jnp.transpose(wkv[..., :NOPE_DIM], (1, 2, 0))
    w['w_uk'] = _block_diag(w_uk).astype(BF16)
    w_uv = jnp.transpose(wkv[..., NOPE_DIM:], (1, 0, 2))
    eye = jnp.eye(N_HEADS, dtype=F32)
    w['w_uv'] = jnp.einsum('hrv,hg->hrgv', w_uv, eye).reshape(N_HEADS, KV_RANK, N_HEADS * V_DIM).astype(BF16)
    w['g_att'] = lp['norm_att_out'].reshape(1, -1)
    w['conv_w'] = lp['conv_w']
    w['conv_b'] = lp['conv_b'].reshape(1, LRU_WIDTH)
    w['w_gx'] = _block_diag(lp['w_gate_x']).astype(BF16)
    w['b_gx'] = lp['b_gate_x'].reshape(1, LRU_WIDTH)
    w['w_ga'] = _block_diag(lp['w_gate_a']).astype(BF16)
    w['b_ga'] = lp['b_gate_a'].reshape(1, LRU_WIDTH)
    w['lam'] = lp['lru_lambda'].reshape(1, LRU_WIDTH)
    w['g_rec'] = lp['norm_rec_out'].reshape(1, LRU_WIDTH)
    n_att = N_HEADS * V_DIM
    w['w_o_att'] = lp['w_o'][:n_att].astype(BF16)
    w['w_o_rec'] = lp['w_o'][n_att:].astype(BF16)
    w['g_mem'] = lp['norm_mem_attn'].reshape(1, d)
    w['g_memtok'] = lp['norm_memtok'].reshape(1, d)
    w['w_mq'] = lp['w_mem_q'].astype(BF16)
    w['w_mkv'] = lp['w_mem_kv'].astype(BF16)
    w['w_mo'] = lp['w_mem_o'].astype(BF16)
    w['g_moe'] = lp['norm_moe'].reshape(1, d)
    w['w_router'] = jnp.pad(lp['w_router'], ((0, 0), (0, LANES - N_EXPERTS))).astype(BF16)
    w['b_router'] = jnp.pad(lp['b_router'], (0, LANES - N_EXPERTS), constant_values=NEG_BIG).reshape(1, LANES)
    w['w_gu'] = lp['w_gate_up']
    w['b_gu'] = lp['b_gate_up'].reshape(N_EXPERTS, 1, 2 * EXPERT_FF)
    w['w_dn'] = lp['w_down']
    w['b_dn'] = lp['b_down'].reshape(N_EXPERTS, 1, d)
    return w


def _rope_tables(pos, rows):
    inv = ROPE_THETA ** (-jnp.arange(ROPE_HALF, dtype=F32) / ROPE_HALF)
    ang = pos.astype(F32)[:, None] * inv[None, :]
    cos, sin = jnp.cos(ang), jnp.sin(ang)
    cos_t = jnp.tile(jnp.concatenate([cos, cos], axis=1), (1, HEADS_PER_LANE_GROUP))
    sin_t = jnp.tile(jnp.concatenate([-sin, sin], axis=1), (1, HEADS_PER_LANE_GROUP))
    reps = rows // pos.shape[0]
    return jnp.tile(cos_t, (reps, 1)), jnp.tile(sin_t, (reps, 1))


def _layer_prompt(x, mem, w):
    batch, seq, d = x.shape
    n = batch * seq
    x2d = x.reshape(n, d)
    cos_t, sin_t = _rope_tables(jnp.arange(seq), seq)
    q, kc, ckv, kr, xb, gb = _pre_call(x2d, cos_t, sin_t, w)
    att = _attn_prompt_call(q, kc, w, batch, seq)
    rec, conv_buf, h_last = _lru_prompt_call(xb, gb, w, batch, seq)
    x1, qm = _mix_out_call(x2d, att, rec, w)
    n_mem = mem.shape[1]
    mk, mv, mkb, mvb = _mem_kv_call(mem.reshape(batch * n_mem, d), w)
    o = _mem_attn_prompt_call(qm, mkb, mvb, batch, seq)
    routed = _router_call(x1, o, jnp.zeros((1, LANES), F32), w)
    caches = (ckv.reshape(batch, seq, KV_RANK), kr.reshape(batch, seq, ROPE_DIM), conv_buf,
              h_last.reshape(batch, LRU_WIDTH), mk.reshape(batch, n_mem, MEM_HEADS, MEM_HEAD_DIM),
              mv.reshape(batch, n_mem, MEM_HEADS, MEM_HEAD_DIM))
    return routed, caches


def _layer_sample(x, page_table, pool_ckv, pool_krope, conv_buf, h0, mk, mv, count0, w):
    batch, n_t, d = x.shape
    n = batch * n_t
    n_past = page_table.shape[1] * PAGE_SIZE
    x2d = x.reshape(n, d)
    cos_t, sin_t = _rope_tables(n_past + jnp.arange(n_t), min(TOKEN_TILE, n))
    q, kc, ckv, kr, xb, gb = _pre_call(x2d, cos_t, sin_t, w)
    q_item = jnp.transpose(q.reshape(N_HEADS, batch, n_t, QK_WIDTH), (1, 0, 2, 3))
    q_item = jnp.pad(q_item, ((0, 0), (0, 0), (0, TOK_PAD - n_t), (0, 0)))
    rope_parts = [q_item[:, h, :, KV_RANK + (h % HEADS_PER_LANE_GROUP) * ROPE_DIM:
                         KV_RANK + (h % HEADS_PER_LANE_GROUP + 1) * ROPE_DIM] for h in range(N_HEADS)]
    q_rope = jnp.stack(rope_parts, axis=1).reshape(batch, N_HEADS * TOK_PAD, ROPE_DIM)
    q_item = q_item.reshape(batch, N_HEADS * TOK_PAD, QK_WIDTH)
    att8 = _attn_sample_call(page_table, q_item, q_rope, kc.reshape(batch, n_t, QK_WIDTH), pool_ckv,
                             jnp.swapaxes(pool_krope, 1, 2), w)
    att = att8[:, :n_t].reshape(n, N_HEADS * V_DIM).astype(BF16)
    tmaj = lambda a: jnp.swapaxes(a, 0, 1)
    rec_t, conv_t, h_last = _lru_sample_call(tmaj(xb.reshape(batch, n_t, LRU_WIDTH)), tmaj(gb.reshape(batch, n_t, LRU_WIDTH)),
                                             tmaj(conv_buf), h0, w)
    rec = tmaj(rec_t).reshape(n, LRU_WIDTH)
    x1, qm = _mix_out_call(x2d, att, rec, w)
    q8 = jnp.pad(qm.reshape(batch, n_t, d).astype(F32), ((0, 0), (0, TOK_PAD - n_t), (0, 0)))
    o8 = _mem_attn_sample_call(q8, mk, mv)
    o = o8[:, :n_t].reshape(n, d).astype(BF16)
    routed = _router_call(x1, o, count0, w)
    caches = (ckv.reshape(batch, n_t, KV_RANK), kr.reshape(batch, n_t, ROPE_DIM), tmaj(conv_t), h_last)
    return routed, caches


def kernel(x_prompt, x_sample, mem_prompt, cache_ckv, cache_krope, cache_conv, state_rglru, cache_mem_k, cache_mem_v, page_table, w_in, norm_mix, norm_q_lat, w_q_b, norm_kv_lat, w_kv_b, conv_w, conv_b, w_gate_x, b_gate_x, w_gate_a, b_gate_a, lru_lambda, norm_att_out, norm_rec_out, w_o, norm_mem_attn, norm_memtok, w_mem_q, w_mem_kv, w_mem_o, norm_moe, w_router, b_router, w_gate_up, b_gate_up, w_down, b_down, norm_final):
    depth = w_in.shape[0]
    assert depth == 1, "the final norm is fused into the last layer's combine kernel"
    g_final = norm_final.reshape(1, -1)
    lp = dict(w_in=w_in[0], norm_mix=norm_mix[0], norm_q_lat=norm_q_lat[0], w_q_b=w_q_b[0],
              norm_kv_lat=norm_kv_lat[0], w_kv_b=w_kv_b[0], conv_w=conv_w[0], conv_b=conv_b[0],
              w_gate_x=w_gate_x[0], b_gate_x=b_gate_x[0], w_gate_a=w_gate_a[0], b_gate_a=b_gate_a[0],
              lru_lambda=lru_lambda[0], norm_att_out=norm_att_out[0], norm_rec_out=norm_rec_out[0],
              w_o=w_o[0], norm_mem_attn=norm_mem_attn[0], norm_memtok=norm_memtok[0],
              w_mem_q=w_mem_q[0], w_mem_kv=w_mem_kv[0], w_mem_o=w_mem_o[0], norm_moe=norm_moe[0],
              w_router=w_router[0], b_router=b_router[0], w_gate_up=w_gate_up[0],
              b_gate_up=b_gate_up[0], w_down=w_down[0], b_down=b_down[0])
    w = _prep_weights(lp)
    routed_p, (ckv_p, kr_p, conv_p, h_p, mk_p, mv_p) = _layer_prompt(x_prompt, mem_prompt, w)
    routed_s, (ckv_s, kr_s, conv_s, h_s) = _layer_sample(x_sample, page_table, cache_ckv[0], cache_krope[0],
                                                         cache_conv[0], state_rglru[0], cache_mem_k[0],
                                                         cache_mem_v[0], routed_p[5], w)
    yp, ys = _moe(routed_p, routed_s, routed_s[5], w, g_final)
    return (yp.reshape(x_prompt.shape), ys.reshape(x_sample.shape), ckv_p[None], kr_p[None], conv_p[None],
            h_p[None], mk_p[None], mv_p[None], ckv_s[None], kr_s[None], conv_s[None], h_s[None])
```

```python
import functools

import jax
import jax.numpy as jnp
from jax import lax
from jax.experimental import pallas as pl
from jax.experimental.pallas import tpu as pltpu

F32 = jnp.float32
BF16 = jnp.bfloat16

N_HEADS = 8
NOPE_DIM = 64
V_DIM = 64
ROPE_DIM = 32
ROPE_HALF = ROPE_DIM // 2
KV_RANK = 128
Q_RANK = 256
LRU_WIDTH = 512
LRU_BLOCKS = 8
CONV_WIDTH = 4
LRU_C = 8.0
MEM_HEADS = 4
MEM_HEAD_DIM = 256
N_EXPERTS = 32
TOP_K = 4
EXPERT_FF = 1024
SWIGLU_LIMIT = 7.0
SWIGLU_ALPHA = 1.702
ROPE_THETA = 10000.0
EPS = 1e-6
MLA_SCALE = (NOPE_DIM + ROPE_DIM) ** -0.5
PAGE_SIZE = 128

LANES = 128
SUBLANES = 8
HEADS_PER_LANE_GROUP = LANES // ROPE_DIM
QK_WIDTH = KV_RANK + LANES
NEG_BIG = -1e30
VMEM_LIMIT = 56 * 1024 * 1024

TOKEN_TILE = 256
ATTN_Q_TILE = 128
ATTN_K_TILE = 512
LRU_TIME_TILE = 256
MOE_TILE = 256
TOK_PAD = SUBLANES


def _params(sem, vmem=VMEM_LIMIT):
    return pltpu.CompilerParams(dimension_semantics=sem, vmem_limit_bytes=vmem)


def _rms(x, g):
    return x * lax.rsqrt(jnp.mean(x * x, axis=-1, keepdims=True) + EPS) * g


def _dot(a, b):
    return jnp.dot(a, b, preferred_element_type=F32)


def _dot_nt(a, b):
    return lax.dot_general(a, b, (((1,), (1,)), ((), ())), preferred_element_type=F32)


C_Q0, C_KV0, C_XB0, C_GB0, C_KR0, C_END = 0, 256, 384, 896, 1408, 1536


def _pre_kernel(x_ref, cos_ref, sin_ref, g_mix_ref, w_in_ref, g_q_ref, w_q_ref, w_uk_ref, g_kv_ref,
                q_ref, kc_ref, kt_ref, va_ref, ckv_ref, kr_ref, xb_ref, gb_ref):
    x = x_ref[...]
    hn = _rms(x, g_mix_ref[...]).astype(BF16)
    proj = _dot(hn, w_in_ref[...])
    cos = cos_ref[...]
    sin = sin_ref[...]

    qn = _rms(proj[:, C_Q0:C_KV0], g_q_ref[...]).astype(BF16)
    qall = _dot(qn, w_q_ref[...])
    n_nope = N_HEADS * NOPE_DIM
    q_lat = _dot(qall[:, :n_nope].astype(BF16), w_uk_ref[...]) * MLA_SCALE
    lane = lax.broadcasted_iota(jnp.int32, (1, LANES), 1)
    for grp in range(N_HEADS // HEADS_PER_LANE_GROUP):
        a = n_nope + grp * LANES
        b = a + N_HEADS * ROPE_DIM
        rot = (qall[:, a:a + LANES] * cos + qall[:, b:b + LANES] * sin) * MLA_SCALE
        for j in range(HEADS_PER_LANE_GROUP):
            h = grp * HEADS_PER_LANE_GROUP + j
            own = (lane >= j * ROPE_DIM) & (lane < (j + 1) * ROPE_DIM)
            q_ref[h, :, :KV_RANK] = q_lat[:, h * KV_RANK:(h + 1) * KV_RANK].astype(BF16)
            q_ref[h, :, KV_RANK:] = jnp.where(own, rot, 0.0).astype(BF16)

    ckv = _rms(proj[:, C_KV0:C_XB0], g_kv_ref[...])
    ckv_ref[...] = ckv
    kx = proj[:, C_KR0:C_END]
    kr = kx * cos + pltpu.roll(kx, ROPE_HALF, axis=1) * sin
    kr_ref[...] = kr[:, :ROPE_DIM]
    kc_ref[:, :KV_RANK] = ckv.astype(BF16)
    kc_ref[:, KV_RANK:] = kr.astype(BF16)
    kt_ref[:KV_RANK, :] = ckv.T.astype(BF16)
    kt_ref[KV_RANK:, :] = kr.T.astype(BF16)
    va_ref[:, :KV_RANK] = ckv.astype(BF16)
    va_ref[:, KV_RANK:] = jnp.broadcast_to(jnp.where(lane == 0, 1.0, 0.0), kr.shape).astype(BF16)
    xb_ref[...] = proj[:, C_XB0:C_GB0]
    gb_ref[...] = proj[:, C_GB0:C_KR0]


def _pre_call(x, cos_t, sin_t, w):
    n, d = x.shape
    tm = min(TOKEN_TILE, n)
    n_pos_blocks = cos_t.shape[0] // tm
    row = lambda i: (i, 0)
    fixed = lambda i: (0, 0)
    pos = lambda i: (i % n_pos_blocks, 0)
    return pl.pallas_call(
        _pre_kernel,
        grid=(n // tm,),
        in_specs=[
            pl.BlockSpec((tm, d), row),
            pl.BlockSpec((tm, LANES), pos),
            pl.BlockSpec((tm, LANES), pos),
            pl.BlockSpec((1, d), fixed),
            pl.BlockSpec((d, C_END), fixed),
            pl.BlockSpec((1, Q_RANK), fixed),
            pl.BlockSpec(w['w_q'].shape, fixed),
            pl.BlockSpec(w['w_uk'].shape, fixed),
            pl.BlockSpec((1, KV_RANK), fixed),
        ],
        out_specs=[
            pl.BlockSpec((N_HEADS, tm, QK_WIDTH), lambda i: (0, i, 0)),
            pl.BlockSpec((tm, QK_WIDTH), row),
            pl.BlockSpec((QK_WIDTH, tm), lambda i: (0, i)),
            pl.BlockSpec((tm, QK_WIDTH), row),
            pl.BlockSpec((tm, KV_RANK), row),
            pl.BlockSpec((tm, ROPE_DIM), row),
            pl.BlockSpec((tm, LRU_WIDTH), row),
            pl.BlockSpec((tm, LRU_WIDTH), row),
        ],
        out_shape=[
            jax.ShapeDtypeStruct((N_HEADS, n, QK_WIDTH), BF16),
            jax.ShapeDtypeStruct((n, QK_WIDTH), BF16),
            jax.ShapeDtypeStruct((QK_WIDTH, n), BF16),
            jax.ShapeDtypeStruct((n, QK_WIDTH), BF16),
            jax.ShapeDtypeStruct((n, KV_RANK), F32),
            jax.ShapeDtypeStruct((n, ROPE_DIM), F32),
            jax.ShapeDtypeStruct((n, LRU_WIDTH), F32),
            jax.ShapeDtypeStruct((n, LRU_WIDTH), F32),
        ],
        compiler_params=_params(("parallel",)),
        name="pre_mixer",
    )(x, cos_t, sin_t, w['g_mix'], w['w_in'], w['g_q'], w['w_q'], w['w_uk'], w['g_kv'])


def _mla_out_norm(acc_of, l_of, w_uv_ref, g_att_ref, rows):
    att = jnp.zeros((rows, N_HEADS * V_DIM), F32)
    for h in range(N_HEADS):
        o_h = (acc_of(h) / l_of(h)).astype(BF16)
        att = att + _dot(o_h, w_uv_ref[h])
    return _rms(att, g_att_ref[...])


def _attn_prompt_kernel(q_ref, kt_ref, va_ref, w_uv_ref, g_att_ref, o_ref, m_ref, acc_ref):
    i = pl.program_id(1)
    tq, tk = ATTN_Q_TILE, ATTN_K_TILE
    m_ref[...] = jnp.full(m_ref.shape, NEG_BIG, F32)
    acc_ref[...] = jnp.zeros(acc_ref.shape, F32)

    def step(j, masked):
        off = pl.multiple_of(j * tk, tk)
        kt = kt_ref[:, pl.ds(off, tk)]
        va = va_ref[pl.ds(off, tk), :]
        if masked:
            q_pos = i * tq + lax.broadcasted_iota(jnp.int32, (tq, 1), 0)
            k_pos = j * tk + lax.broadcasted_iota(jnp.int32, (1, tk), 1)
            keep = k_pos <= q_pos
        scores = [_dot(q_ref[h], kt) for h in range(N_HEADS)]
        for h in range(N_HEADS):
            s = scores[h]
            if masked:
                s = jnp.where(keep, s, NEG_BIG)
            m_prev = m_ref[h]
            m_new = jnp.maximum(m_prev, jnp.max(s, axis=1, keepdims=True))
            alpha = jnp.exp(m_prev - m_new)
            p = jnp.exp(s - m_new)
            acc_ref[h] = alpha * acc_ref[h] + _dot(p.astype(BF16), va)
            m_ref[h] = m_new

    n_full = (i * tq) // tk

    def body(j, carry):
        step(j, False)
        return carry

    lax.fori_loop(0, n_full, body, 0)
    step(n_full, True)
    o_ref[...] = _mla_out_norm(lambda h: acc_ref[h, :, :KV_RANK], lambda h: acc_ref[h, :, KV_RANK:KV_RANK + 1],
                               w_uv_ref, g_att_ref, tq).astype(BF16)


def _attn_prompt_call(q, kt, va, w, batch, seq):
    tq = ATTN_Q_TILE
    nq = seq // tq
    return pl.pallas_call(
        _attn_prompt_kernel,
        grid=(batch, nq),
        in_specs=[
            pl.BlockSpec((N_HEADS, tq, QK_WIDTH), lambda b, i: (0, b * nq + i, 0)),
            pl.BlockSpec((QK_WIDTH, seq), lambda b, i: (0, b)),
            pl.BlockSpec((seq, QK_WIDTH), lambda b, i: (b, 0)),
            pl.BlockSpec(w['w_uv'].shape, lambda b, i: (0, 0, 0)),
            pl.BlockSpec((1, N_HEADS * V_DIM), lambda b, i: (0, 0)),
        ],
        out_specs=pl.BlockSpec((tq, N_HEADS * V_DIM), lambda b, i: (b * nq + i, 0)),
        out_shape=jax.ShapeDtypeStruct((batch * seq, N_HEADS * V_DIM), BF16),
        scratch_shapes=[
            pltpu.VMEM((N_HEADS, tq, 1), F32),
            pltpu.VMEM((N_HEADS, tq, QK_WIDTH), F32),
        ],
        compiler_params=_params(("parallel", "parallel")),
        name="mla_prompt",
    )(q, kt, va, w['w_uv'], w['g_att'])


def _attn_sample_kernel(pt_ref, q_ref, qr_ref, kc_ref, w_uv_ref, g_att_ref, ckv_hbm, kr_hbm,
                        o_ref, ckv_buf, kr_buf, sems, l_ref, acc_ref):
    b = pl.program_id(0)
    nb = pl.num_programs(0)
    n_pages = ckv_buf.shape[1]
    rows = N_HEADS * TOK_PAD

    def copies(item, slot, page):
        pid = pt_ref[item, page]
        lanes = pl.ds(pl.multiple_of(page * PAGE_SIZE, PAGE_SIZE), PAGE_SIZE)
        return (pltpu.make_async_copy(ckv_hbm.at[pid], ckv_buf.at[slot, page], sems.at[0, slot]),
                pltpu.make_async_copy(kr_hbm.at[pid], kr_buf.at[slot, :, lanes], sems.at[1, slot]))

    def start_item(item, slot):
        def go(page, carry):
            c0, c1 = copies(item, slot, page)
            c0.start()
            c1.start()
            return carry
        lax.fori_loop(0, n_pages, go, 0)

    def wait_item(item, slot):
        def go(page, carry):
            c0, c1 = copies(item, slot, page)
            c0.wait()
            c1.wait()
            return carry
        lax.fori_loop(0, n_pages, go, 0)

    slot = b % 2

    @pl.when(b == 0)
    def _():
        start_item(0, 0)

    @pl.when(b + 1 < nb)
    def _():
        start_item(b + 1, 1 - slot)

    wait_item(b, slot)

    q = q_ref[0]
    q_lat = q[:, :KV_RANK]
    q_rope = qr_ref[0]
    n_keys = n_pages * PAGE_SIZE
    ckv = ckv_buf[slot].reshape(n_keys, KV_RANK).astype(BF16)
    kr_t = kr_buf[slot].astype(BF16)
    s_past = _dot_nt(q_lat, ckv) + _dot(q_rope, kr_t)

    kc_new = kc_ref[0].astype(F32)
    n_new = kc_new.shape[0]
    qf = q.astype(F32)
    tok = lax.broadcasted_iota(jnp.int32, (rows, 1), 0) & (TOK_PAD - 1)
    s_new = []
    m = jnp.max(s_past, axis=1, keepdims=True)
    for t in range(n_new):
        s_t = jnp.sum(qf * kc_new[t:t + 1, :], axis=1, keepdims=True)
        s_t = jnp.where(tok >= t, s_t, NEG_BIG)
        s_new.append(s_t)
        m = jnp.maximum(m, s_t)
    p_past = jnp.exp(s_past - m)
    l = jnp.sum(p_past, axis=1, keepdims=True)
    acc = _dot(p_past.astype(BF16), ckv)
    for t in range(n_new):
        p_t = jnp.exp(s_new[t] - m)
        l = l + p_t
        acc = acc + p_t.astype(BF16).astype(F32) * kc_new[t:t + 1, :KV_RANK]
    l_ref[...] = l
    acc_ref[...] = acc
    rows_of = lambda h: slice(h * TOK_PAD, (h + 1) * TOK_PAD)
    o_ref[0] = _mla_out_norm(lambda h: acc_ref[rows_of(h), :], lambda h: l_ref[rows_of(h), :],
                             w_uv_ref, g_att_ref, TOK_PAD)


def _attn_sample_call(page_table, q, q_rope, kc_new, cache_ckv, cache_krope, w):
    n_items, n_pages = page_table.shape
    rows = N_HEADS * TOK_PAD
    n_new = kc_new.shape[1]
    grid_spec = pltpu.PrefetchScalarGridSpec(
        num_scalar_prefetch=1,
        grid=(n_items,),
        in_specs=[
            pl.BlockSpec((1, rows, QK_WIDTH), lambda b, pt: (b, 0, 0)),
            pl.BlockSpec((1, rows, ROPE_DIM), lambda b, pt: (b, 0, 0)),
            pl.BlockSpec((1, n_new, QK_WIDTH), lambda b, pt: (b, 0, 0)),
            pl.BlockSpec(w['w_uv'].shape, lambda b, pt: (0, 0, 0)),
            pl.BlockSpec((1, N_HEADS * V_DIM), lambda b, pt: (0, 0)),
            pl.BlockSpec(memory_space=pl.ANY),
            pl.BlockSpec(memory_space=pl.ANY),
        ],
        out_specs=pl.BlockSpec((1, TOK_PAD, N_HEADS * V_DIM), lambda b, pt: (b, 0, 0)),
        scratch_shapes=[
            pltpu.VMEM((2, n_pages, PAGE_SIZE, KV_RANK), F32),
            pltpu.VMEM((2, ROPE_DIM, n_pages * PAGE_SIZE), F32),
            pltpu.SemaphoreType.DMA((2, 2)),
            pltpu.VMEM((rows, 1), F32),
            pltpu.VMEM((rows, KV_RANK), F32),
        ],
    )
    return pl.pallas_call(
        _attn_sample_kernel,
        grid_spec=grid_spec,
        out_shape=jax.ShapeDtypeStruct((n_items, TOK_PAD, N_HEADS * V_DIM), F32),
        compiler_params=_params(("arbitrary",)),
        name="mla_sample",
    )(page_table, q, q_rope, kc_new, w['w_uv'], w['g_att'], cache_ckv, cache_krope)


def _lru_gates(xc, gb, w):
    (wgx_ref, bgx_ref, wga_ref, bga_ref, lam_ref) = w
    xcb = xc.astype(BF16)
    gate_x = jax.nn.sigmoid(_dot(xcb, wgx_ref[...]) + bgx_ref[...])
    gate_a = jax.nn.sigmoid(_dot(xcb, wga_ref[...]) + bga_ref[...])
    neg_lam = -lam_ref[...]
    softplus = jnp.maximum(neg_lam, 0.0) + jnp.log1p(jnp.exp(-jnp.abs(neg_lam)))
    log_a = -LRU_C * gate_a * softplus
    a = jnp.exp(log_a)
    u = jnp.sqrt(1.0 - jnp.exp(2.0 * log_a)) * (gate_x * xc)
    return a, u, jax.nn.gelu(gb)


def _lru_prompt_kernel(xb_ref, gb_ref, cw_ref, cb_ref, wgx_ref, bgx_ref, wga_ref, bga_ref, lam_ref,
                       g_rec_ref, rec_ref, conv_ref, hlast_ref, xs_ref, h_ref):
    t = pl.program_id(1)
    tt = LRU_TIME_TILE
    halo = SUBLANES

    @pl.when(t == 0)
    def _():
        xs_ref[0:halo, :] = jnp.zeros((halo, LRU_WIDTH), F32)
        h_ref[...] = jnp.zeros((1, LRU_WIDTH), F32)

    xb = xb_ref[...]
    xs_ref[halo:, :] = xb
    xc = cb_ref[...] + xb * cw_ref[CONV_WIDTH - 1:CONV_WIDTH, :]
    for d in range(1, CONV_WIDTH):
        xc = xc + xs_ref[pl.ds(halo - d, tt), :] * cw_ref[CONV_WIDTH - 1 - d:CONV_WIDTH - d, :]
    xs_ref[0:halo, :] = xb[tt - halo:, :]

    a, u, gate = _lru_gates(xc, gb_ref[...], (wgx_ref, bgx_ref, wga_ref, bga_ref, lam_ref))

    row = lax.broadcasted_iota(jnp.int32, (tt, 1), 0)
    d = 1
    while d < tt:
        if d < SUBLANES:
            a_sh = pltpu.roll(a, d, axis=0)
            u_sh = pltpu.roll(u, d, axis=0)
            keep = row >= d
            u = jnp.where(keep, a * u_sh + u, u)
            a = jnp.where(keep, a * a_sh, a)
        else:
            a_sh = jnp.concatenate([jnp.ones((d, LRU_WIDTH), F32), a[:tt - d]], axis=0)
            u_sh = jnp.concatenate([jnp.zeros((d, LRU_WIDTH), F32), u[:tt - d]], axis=0)
            u = a * u_sh + u
            a = a * a_sh
        d *= 2
    h = a * h_ref[...] + u
    h_ref[...] = h[tt - 1:tt, :]
    rec_ref[...] = _rms(h * gate, g_rec_ref[...]).astype(BF16)
    conv_ref[0] = xb[tt - (CONV_WIDTH - 1):, :]
    hlast_ref[0] = h[tt - 1:tt, :]


def _lru_prompt_call(xb, gb, w, batch, seq):
    tt = LRU_TIME_TILE
    nt = seq // tt
    row = lambda b, t: (b * nt + t, 0)
    fixed = lambda b, t: (0, 0)
    wid = LRU_WIDTH
    return pl.pallas_call(
        _lru_prompt_kernel,
        grid=(batch, nt),
        in_specs=[
            pl.BlockSpec((tt, wid), row),
            pl.BlockSpec((tt, wid), row),
            pl.BlockSpec((CONV_WIDTH, wid), fixed),
            pl.BlockSpec((1, wid), fixed),
            pl.BlockSpec((wid, wid), fixed),
            pl.BlockSpec((1, wid), fixed),
            pl.BlockSpec((wid, wid), fixed),
            pl.BlockSpec((1, wid), fixed),
            pl.BlockSpec((1, wid), fixed),
            pl.BlockSpec((1, wid), fixed),
        ],
        out_specs=[
            pl.BlockSpec((tt, wid), row),
            pl.BlockSpec((1, CONV_WIDTH - 1, wid), lambda b, t: (b, 0, 0)),
            pl.BlockSpec((1, 1, wid), lambda b, t: (b, 0, 0)),
        ],
        out_shape=[
            jax.ShapeDtypeStruct((batch * seq, wid), BF16),
            jax.ShapeDtypeStruct((batch, CONV_WIDTH - 1, wid), F32),
            jax.ShapeDtypeStruct((batch, 1, wid), F32),
        ],
        scratch_shapes=[
            pltpu.VMEM((tt + SUBLANES, wid), F32),
            pltpu.VMEM((1, wid), F32),
        ],
        compiler_params=_params(("parallel", "arbitrary")),
        name="rglru_prompt",
    )(xb, gb, w['conv_w'], w['conv_b'], w['w_gx'], w['b_gx'], w['w_ga'], w['b_ga'], w['lam'], w['g_rec'])


def _lru_sample_kernel(xb_ref, gb_ref, cbuf_ref, h0_ref, cw_ref, cb_ref, wgx_ref, bgx_ref, wga_ref,
                       bga_ref, lam_ref, g_rec_ref, rec_ref, conv_ref, hlast_ref):
    n_t = xb_ref.shape[0]
    n_b = xb_ref.shape[1]
    xpad = [cbuf_ref[j] for j in range(CONV_WIDTH - 1)] + [xb_ref[t] for t in range(n_t)]
    xcs = []
    for t in range(n_t):
        xc = cb_ref[...]
        for j in range(CONV_WIDTH):
            xc = xc + xpad[t + j] * cw_ref[j:j + 1, :]
        xcs.append(xc)
    xc_all = jnp.concatenate(xcs, axis=0)
    gb_all = jnp.concatenate([gb_ref[t] for t in range(n_t)], axis=0)
    a, u, gate = _lru_gates(xc_all, gb_all, (wgx_ref, bgx_ref, wga_ref, bga_ref, lam_ref))
    h = h0_ref[...]
    for t in range(n_t):
        sl = slice(t * n_b, (t + 1) * n_b)
        h = a[sl] * h + u[sl]
        rec_ref[t] = _rms(h * gate[sl], g_rec_ref[...]).astype(BF16)
    for j in range(CONV_WIDTH - 1):
        conv_ref[j] = xpad[n_t + j]
    hlast_ref[...] = h


def _lru_sample_call(xb_t, gb_t, cbuf_t, h0, w):
    n_t, n_b, wid = xb_t.shape
    return pl.pallas_call(
        _lru_sample_kernel,
        out_shape=[
            jax.ShapeDtypeStruct((n_t, n_b, wid), BF16),
            jax.ShapeDtypeStruct((CONV_WIDTH - 1, n_b, wid), F32),
            jax.ShapeDtypeStruct((n_b, wid), F32),
        ],
        compiler_params=pltpu.CompilerParams(vmem_limit_bytes=VMEM_LIMIT),
        name="rglru_sample",
    )(xb_t, gb_t, cbuf_t, h0, w['conv_w'], w['conv_b'], w['w_gx'], w['b_gx'], w['w_ga'], w['b_ga'],
      w['lam'], w['g_rec'])


def _mix_out_kernel(x_ref, att_ref, rec_ref, w_oa_ref, w_or_ref, g_mem_ref, w_mq_ref, x1_ref, qm_ref):
    x1 = x_ref[...] + _dot(att_ref[...], w_oa_ref[...]) + _dot(rec_ref[...], w_or_ref[...])
    x1_ref[...] = x1
    hn = _rms(x1, g_mem_ref[...]).astype(BF16)
    qm_ref[...] = (_dot(hn, w_mq_ref[...]) * (MEM_HEAD_DIM ** -0.5)).astype(BF16)


def _mix_out_call(x, att, rec, w):
    n, d = x.shape
    tm = min(TOKEN_TILE, n)
    half = att.shape[1]
    row = lambda i: (i, 0)
    fixed = lambda i: (0, 0)
    return pl.pallas_call(
        _mix_out_kernel,
        grid=(n // tm,),
        in_specs=[
            pl.BlockSpec((tm, d), row),
            pl.BlockSpec((tm, half), row),
            pl.BlockSpec((tm, half), row),
            pl.BlockSpec((half, d), fixed),
            pl.BlockSpec((half, d), fixed),
            pl.BlockSpec((1, d), fixed),
            pl.BlockSpec((d, d), fixed),
        ],
        out_specs=[pl.BlockSpec((tm, d), row), pl.BlockSpec((tm, d), row)],
        out_shape=[jax.ShapeDtypeStruct((n, d), F32), jax.ShapeDtypeStruct((n, d), BF16)],
        compiler_params=_params(("parallel",)),
        name="mix_out",
    )(x, att, rec, w['w_o_att'], w['w_o_rec'], w['g_mem'], w['w_mq'])


def _mem_kv_kernel(mem_ref, g_ref, w_ref, k_ref, v_ref, kb_ref, vb_ref):
    d = mem_ref.shape[1]
    kv = _dot(_rms(mem_ref[...], g_ref[...]).astype(BF16), w_ref[...])
    k_ref[...] = kv[:, :d]
    v_ref[...] = kv[:, d:]
    kb_ref[...] = kv[:, :d].astype(BF16)
    vb_ref[...] = kv[:, d:].astype(BF16)


def _mem_kv_call(mem, w):
    n, d = mem.shape
    tm = min(TOKEN_TILE, n)
    row = lambda i: (i, 0)
    fixed = lambda i: (0, 0)
    return pl.pallas_call(
        _mem_kv_kernel,
        grid=(n // tm,),
        in_specs=[pl.BlockSpec((tm, d), row), pl.BlockSpec((1, d), fixed), pl.BlockSpec((d, 2 * d), fixed)],
        out_specs=[pl.BlockSpec((tm, d), row)] * 4,
        out_shape=[jax.ShapeDtypeStruct((n, d), F32)] * 2 + [jax.ShapeDtypeStruct((n, d), BF16)] * 2,
        compiler_params=_params(("parallel",)),
        name="mem_kv",
    )(mem, w['g_memtok'], w['w_mkv'])


def _mem_attn_prompt_kernel(q_ref, k_ref, v_ref, o_ref):
    for h in range(MEM_HEADS):
        sl = slice(h * MEM_HEAD_DIM, (h + 1) * MEM_HEAD_DIM)
        s = _dot_nt(q_ref[:, sl], k_ref[:, sl])
        p = jnp.exp(s - jnp.max(s, axis=1, keepdims=True))
        p = p / jnp.sum(p, axis=1, keepdims=True)
        o_ref[:, sl] = _dot(p.astype(BF16), v_ref[:, sl]).astype(BF16)


def _mem_attn_prompt_call(qm, kb, vb, batch, seq):
    n, d = qm.shape
    tm = min(TOKEN_TILE, n)
    nt = seq // tm
    n_mem = kb.shape[0] // batch
    return pl.pallas_call(
        _mem_attn_prompt_kernel,
        grid=(batch, nt),
        in_specs=[
            pl.BlockSpec((tm, d), lambda b, i: (b * nt + i, 0)),
            pl.BlockSpec((n_mem, d), lambda b, i: (b, 0)),
            pl.BlockSpec((n_mem, d), lambda b, i: (b, 0)),
        ],
        out_specs=pl.BlockSpec((tm, d), lambda b, i: (b * nt + i, 0)),
        out_shape=jax.ShapeDtypeStruct((n, d), BF16),
        compiler_params=_params(("parallel", "parallel")),
        name="mem_attn_prompt",
    )(qm, kb, vb)


def _mem_attn_sample_kernel(q_ref, k_hbm, v_hbm, o_ref, kbuf, vbuf, sems):
    b = pl.program_id(0)
    nb = pl.num_programs(0)

    def copies(item, slot):
        cps = []
        for h in range(MEM_HEADS):
            cps.append(pltpu.make_async_copy(k_hbm.at[item, :, h, :], kbuf.at[slot, h], sems.at[0, slot]))
            cps.append(pltpu.make_async_copy(v_hbm.at[item, :, h, :], vbuf.at[slot, h], sems.at[1, slot]))
        return cps

    slot = b % 2

    @pl.when(b == 0)
    def _():
        for cp in copies(0, 0):
            cp.start()

    @pl.when(b + 1 < nb)
    def _():
        for cp in copies(b + 1, 1 - slot):
            cp.start()

    for cp in copies(b, slot):
        cp.wait()

    q = q_ref[0]
    for h in range(MEM_HEADS):
        sl = slice(h * MEM_HEAD_DIM, (h + 1) * MEM_HEAD_DIM)
        s = _dot_nt(q[:, sl].astype(BF16), kbuf[slot, h].astype(BF16))
        p = jnp.exp(s - jnp.max(s, axis=1, keepdims=True))
        p = p / jnp.sum(p, axis=1, keepdims=True)
        o_ref[0, :, sl] = _dot(p.astype(BF16), vbuf[slot, h].astype(BF16))


def _mem_attn_sample_call(q8, k, v):
    n_items, _, d = q8.shape
    n_mem = k.shape[1]
    blk = lambda i: (i, 0, 0)
    return pl.pallas_call(
        _mem_attn_sample_kernel,
        grid=(n_items,),
        in_specs=[
            pl.BlockSpec((1, TOK_PAD, d), blk),
            pl.BlockSpec(memory_space=pl.ANY),
            pl.BlockSpec(memory_space=pl.ANY),
        ],
        out_specs=pl.BlockSpec((1, TOK_PAD, d), blk),
        out_shape=jax.ShapeDtypeStruct((n_items, TOK_PAD, d), F32),
        scratch_shapes=[
            pltpu.VMEM((2, MEM_HEADS, n_mem, MEM_HEAD_DIM), F32),
            pltpu.VMEM((2, MEM_HEADS, n_mem, MEM_HEAD_DIM), F32),
            pltpu.SemaphoreType.DMA((2, 2)),
        ],
        compiler_params=_params(("arbitrary",)),
        name="mem_attn_sample",
    )(q8, k, v)


def _router_kernel(x1_ref, o_ref, w_mo_ref, g_moe_ref, w_r_ref, b_r_ref, cnt0_ref,
                   x2_ref, hn_ref, idx_ref, gate_ref, rank_ref, cnt_ref, run_ref):
    @pl.when(pl.program_id(0) == 0)
    def _():
        run_ref[...] = cnt0_ref[...]

    x2 = x1_ref[...] + _dot(o_ref[...], w_mo_ref[...])
    x2_ref[...] = x2
    hn_f32 = _rms(x2, g_moe_ref[...])
    hn_ref[...] = hn_f32
    hn = hn_f32.astype(BF16)

    vals = _dot(hn, w_r_ref[...]) + b_r_ref[...]
    tm = vals.shape[0]
    lane = lax.broadcasted_iota(jnp.int32, vals.shape, 1)
    lane_f = lane.astype(F32)
    idx_out = jnp.zeros(vals.shape, F32)
    top, hot = [], []
    for k in range(TOP_K):
        m = jnp.max(vals, axis=1, keepdims=True)
        idx = jnp.min(jnp.where(vals == m, lane_f, float(LANES)), axis=1, keepdims=True)
        idx_out = jnp.where(lane == k, idx, idx_out)
        sel = lane_f == idx
        vals = jnp.where(sel, -jnp.inf, vals)
        top.append(m)
        hot.append(sel)
    e = [jnp.exp(t - top[0]) for t in top]
    denom = e[0] + e[1] + e[2] + e[3]
    gate_out = jnp.zeros(vals.shape, F32)
    for k in range(TOP_K):
        gate_out = jnp.where(lane == k, e[k] / denom, gate_out)
    idx_ref[...] = idx_out.astype(jnp.int32)
    gate_ref[...] = gate_out

    picked = jnp.where(hot[0] | hot[1] | hot[2] | hot[3], 1.0, 0.0)
    earlier = (lax.broadcasted_iota(jnp.int32, (tm, tm), 0) > lax.broadcasted_iota(jnp.int32, (tm, tm), 1))
    before = run_ref[...] + _dot(jnp.where(earlier, 1.0, 0.0).astype(BF16), picked.astype(BF16))
    rank_out = jnp.zeros(vals.shape, F32)
    for k in range(TOP_K):
        rank_k = jnp.sum(jnp.where(hot[k], before, 0.0), axis=1, keepdims=True)
        rank_out = jnp.where(lane == k, rank_k, rank_out)
    rank_ref[...] = rank_out.astype(jnp.int32)
    run_ref[...] = run_ref[...] + jnp.sum(picked, axis=0, keepdims=True)
    cnt_ref[...] = run_ref[...]


def _router_call(x1, o, count0, w):
    n, d = x1.shape
    tm = min(TOKEN_TILE, n)
    row = lambda i: (i, 0)
    fixed = lambda i: (0, 0)
    return pl.pallas_call(
        _router_kernel,
        grid=(n // tm,),
        in_specs=[
            pl.BlockSpec((tm, d), row),
            pl.BlockSpec((tm, d), row),
            pl.BlockSpec((d, d), fixed),
            pl.BlockSpec((1, d), fixed),
            pl.BlockSpec((d, LANES), fixed),
            pl.BlockSpec((1, LANES), fixed),
            pl.BlockSpec((1, LANES), fixed),
        ],
        out_specs=[pl.BlockSpec((tm, d), row), pl.BlockSpec((tm, d), row),
                   pl.BlockSpec((tm, LANES), row), pl.BlockSpec((tm, LANES), row),
                   pl.BlockSpec((tm, LANES), row), pl.BlockSpec((1, LANES), fixed)],
        out_shape=[jax.ShapeDtypeStruct((n, d), F32), jax.ShapeDtypeStruct((n, d), F32),
                   jax.ShapeDtypeStruct((n, LANES), jnp.int32), jax.ShapeDtypeStruct((n, LANES), F32),
                   jax.ShapeDtypeStruct((n, LANES), jnp.int32), jax.ShapeDtypeStruct((1, LANES), F32)],
        scratch_shapes=[pltpu.VMEM((1, LANES), F32)],
        compiler_params=_params(("arbitrary",)),
        name="router",
    )(x1, o, w['w_mo'], w['g_moe'], w['w_router'], w['b_router'], count0)


def _row_copies(dest_ref, n_tok, make, start):
    def body(t, carry):
        for k in range(TOP_K):
            cp = make(t, k, dest_ref[0, 0, t * TOP_K + k])
            if start:
                cp.start()
            else:
                cp.wait()
        return carry
    lax.fori_loop(0, n_tok, body, 0, unroll=4)


def _dispatch_kernel(n_prompt_tiles, pend_ref, dest_ref, hp_ref, hs_ref, xs_ref, zero_ref, sem, zsem):
    i = pl.program_id(0)
    n_tok = hp_ref.shape[0]
    tm = zero_ref.shape[0]

    @pl.when(i == 0)
    def _():
        zero_ref[...] = jnp.zeros(zero_ref.shape, zero_ref.dtype)
        for start in (True, False):
            for e in range(N_EXPERTS):
                end = pend_ref[e]
                begin = pend_ref[e - 1] if e else 0

                @pl.when(end > begin)
                def _():
                    cp = pltpu.make_async_copy(zero_ref, xs_ref.at[pl.ds(pl.multiple_of(end - tm, tm), tm)], zsem)
                    if start:
                        cp.start()
                    else:
                        cp.wait()

        def fill(blk, carry):
            cp = pltpu.make_async_copy(zero_ref, xs_ref.at[pl.ds(pl.multiple_of(blk * tm, tm), tm)], zsem)
            cp.start()
            cp.wait()
            return carry
        lax.fori_loop(pend_ref[N_EXPERTS - 1] // tm, xs_ref.shape[0] // tm, fill, 0)

    def scatter(src_ref):
        make = lambda t, k, row: pltpu.make_async_copy(src_ref.at[pl.ds(t, 1)], xs_ref.at[pl.ds(row, 1)], sem)
        _row_copies(dest_ref, n_tok, make, True)
        _row_copies(dest_ref, n_tok, make, False)

    @pl.when(i < n_prompt_tiles)
    def _():
        scatter(hp_ref)

    @pl.when(i >= n_prompt_tiles)
    def _():
        scatter(hs_ref)


def _dispatch_call(pends, dest, hn_p, hn_s, n_rows):
    n_tiles, _, per_tile = dest.shape
    td = per_tile // TOP_K
    half = hn_p.shape[1]
    n_prompt_tiles = hn_p.shape[0] // td
    n_sample_tiles = hn_s.shape[0] // td
    kernel_fn = functools.partial(_dispatch_kernel, n_prompt_tiles)
    grid_spec = pltpu.PrefetchScalarGridSpec(
        num_scalar_prefetch=1,
        grid=(n_tiles,),
        in_specs=[
            pl.BlockSpec((1, 1, per_tile), lambda i, pe: (i, 0, 0), memory_space=pltpu.SMEM),
            pl.BlockSpec((td, half), lambda i, pe: (jnp.minimum(i, n_prompt_tiles - 1), 0)),
            pl.BlockSpec((td, half), lambda i, pe: (jnp.clip(i - n_prompt_tiles, 0, n_sample_tiles - 1), 0)),
        ],
        out_specs=pl.BlockSpec(memory_space=pl.ANY),
        scratch_shapes=[pltpu.VMEM((MOE_TILE, half), F32), pltpu.SemaphoreType.DMA(()),
                        pltpu.SemaphoreType.DMA(())],
    )
    return pl.pallas_call(
        kernel_fn,
        grid_spec=grid_spec,
        out_shape=jax.ShapeDtypeStruct((n_rows, half), F32),
        compiler_params=_params(("arbitrary",)),
        name="dispatch",
    )(pends, dest, hn_p, hn_s)


def _expert_kernel(be_ref, nb_ref, x_ref, wgu_ref, bgu_ref, wdn_ref, bdn_ref, o_ref, wgu_bf, wdn_bf):
    i = pl.program_id(0)
    used = i < nb_ref[0]
    new_expert = (i == 0) | (be_ref[i] != be_ref[jnp.maximum(i - 1, 0)])
    chunk = 128

    @pl.when(used & new_expert)
    def _():
        def cast(c, carry):
            rows = pl.ds(pl.multiple_of(c * chunk, chunk), chunk)
            wgu_bf[rows, :] = wgu_ref[0, rows, :].astype(BF16)
            wdn_bf[rows, :] = wdn_ref[0, rows, :].astype(BF16)
            return carry
        lax.fori_loop(0, wgu_bf.shape[0] // chunk, cast, 0)

    @pl.when(used)
    def _():
        gu = _dot(x_ref[...].astype(BF16), wgu_bf[...]) + bgu_ref[0]
        gate = jnp.minimum(gu[:, :EXPERT_FF], SWIGLU_LIMIT)
        lin = jnp.clip(gu[:, EXPERT_FF:], -SWIGLU_LIMIT, SWIGLU_LIMIT)
        act = ((lin + 1.0) * gate * jax.nn.sigmoid(SWIGLU_ALPHA * gate)).astype(BF16)
        o_ref[...] = _dot(act, wdn_bf[...]) + bdn_ref[0]

    @pl.when(jnp.logical_not(used))
    def _():
        o_ref[...] = jnp.zeros(o_ref.shape, F32)


def _expert_call(block_expert, n_used, xs, w):
    n_rows, d = xs.shape
    tm = MOE_TILE
    ff2 = 2 * EXPERT_FF
    grid_spec = pltpu.PrefetchScalarGridSpec(
        num_scalar_prefetch=2,
        grid=(n_rows // tm,),
        in_specs=[
            pl.BlockSpec((tm, d), lambda i, be, nb: (jnp.minimum(i, nb[0] - 1), 0)),
            pl.BlockSpec((1, d, ff2), lambda i, be, nb: (be[i], 0, 0)),
            pl.BlockSpec((1, 1, ff2), lambda i, be, nb: (be[i], 0, 0)),
            pl.BlockSpec((1, EXPERT_FF, d), lambda i, be, nb: (be[i], 0, 0)),
            pl.BlockSpec((1, 1, d), lambda i, be, nb: (be[i], 0, 0)),
        ],
        out_specs=pl.BlockSpec((tm, d), lambda i, be, nb: (i, 0)),
        scratch_shapes=[pltpu.VMEM((d, ff2), BF16), pltpu.VMEM((EXPERT_FF, d), BF16)],
    )
    return pl.pallas_call(
        _expert_kernel,
        grid_spec=grid_spec,
        out_shape=jax.ShapeDtypeStruct((n_rows, d), F32),
        compiler_params=_params(("arbitrary",)),
        name="experts",
    )(block_expert, n_used, xs, w['w_gu'], w['b_gu'], w['w_dn'], w['b_dn'])


def _combine_kernel(n, dcur_ref, dnext_ref, x2_ref, gate_ref, g_ref, rows_hbm, o_ref, gbuf, sems):
    i = pl.program_id(0)
    n_tok = x2_ref.shape[0]
    slot = i % 2

    def gather(dref, slot, start):
        make = lambda t, k, row: pltpu.make_async_copy(rows_hbm.at[pl.ds(row, 1)],
                                                       gbuf.at[slot, k, pl.ds(t, 1)], sems.at[slot])
        _row_copies(dref, n_tok, make, start)

    @pl.when(i == 0)
    def _():
        gather(dcur_ref, 0, True)

    @pl.when(i + 1 < n)
    def _():
        gather(dnext_ref, 1 - slot, True)

    gather(dcur_ref, slot, False)
    gates = gate_ref[...]
    y = gbuf[slot, 0] * gates[:, 0:1]
    for k in range(1, TOP_K):
        y = y + gbuf[slot, k] * gates[:, k:k + 1]
    o_ref[...] = _rms(x2_ref[...] + y, g_ref[...])


def _combine_call(dest, tile0, x2, gates, g_final, out_rows):
    n, d = x2.shape
    per_tile = dest.shape[2]
    td = per_tile // TOP_K
    n_tiles = n // td
    row = lambda i: (i, 0)
    return pl.pallas_call(
        functools.partial(_combine_kernel, n_tiles),
        grid=(n_tiles,),
        in_specs=[
            pl.BlockSpec((1, 1, per_tile), lambda i: (tile0 + i, 0, 0), memory_space=pltpu.SMEM),
            pl.BlockSpec((1, 1, per_tile), lambda i: (tile0 + jnp.minimum(i + 1, n_tiles - 1), 0, 0),
                         memory_space=pltpu.SMEM),
            pl.BlockSpec((td, d), row),
            pl.BlockSpec((td, LANES), row),
            pl.BlockSpec((1, d), lambda i: (0, 0)),
            pl.BlockSpec(memory_space=pl.ANY),
        ],
        out_specs=pl.BlockSpec((td, d), row),
        out_shape=jax.ShapeDtypeStruct((n, d), F32),
        scratch_shapes=[pltpu.VMEM((2, TOP_K, td, d), F32), pltpu.SemaphoreType.DMA((2,))],
        compiler_params=_params(("arbitrary",)),
        name="combine",
    )(dest, dest, x2, gates, g_final, out_rows)


def _moe(routed_p, routed_s, counts, w, g_final):
    tm = MOE_TILE
    n_p, n_s = routed_p[0].shape[0], routed_s[0].shape[0]
    td = min(TOKEN_TILE, n_p, n_s)
    nk = (n_p + n_s) * TOP_K
    counts = counts[0, :N_EXPERTS].astype(jnp.int32)
    padded = ((counts + tm - 1) // tm) * tm
    pends = jnp.cumsum(padded)
    pstarts = pends - padded
    idx = jnp.concatenate([routed_p[2][:, :TOP_K], routed_s[2][:, :TOP_K]], axis=0)
    rank = jnp.concatenate([routed_p[4][:, :TOP_K], routed_s[4][:, :TOP_K]], axis=0)
    hot = idx[:, :, None] == jnp.arange(N_EXPERTS, dtype=jnp.int32)[None, None, :]
    dest = rank + jnp.sum(jnp.where(hot, pstarts[None, None, :], 0), axis=-1)
    dest = dest.astype(jnp.int32).reshape((n_p + n_s) // td, 1, td * TOP_K)
    n_blocks = -(-nk // tm) + N_EXPERTS
    block_start = jnp.arange(n_blocks, dtype=jnp.int32) * tm
    block_e = jnp.minimum(jnp.sum(block_start[:, None] >= pends[None, :], axis=1), N_EXPERTS - 1)
    n_used = (pends[-1] // tm).astype(jnp.int32).reshape(1)
    xs = _dispatch_call(pends.astype(jnp.int32), dest, routed_p[1], routed_s[1], n_blocks * tm)
    out_rows = _expert_call(block_e.astype(jnp.int32), n_used, xs, w)
    y_p = _combine_call(dest, 0, routed_p[0], routed_p[3], g_final, out_rows)
    y_s = _combine_call(dest, n_p // td, routed_s[0], routed_s[3], g_final, out_rows)
    return y_p, y_s


def _block_diag(blocks):
    nb, r, c = blocks.shape
    eye = jnp.eye(nb, dtype=blocks.dtype)
    return jnp.einsum('nrc,nm->nrmc', blocks, eye).reshape(nb * r, nb * c)


def _prep_weights(lp):
    w = {}
    d = lp['w_in'].shape[0]
    c0 = Q_RANK
    c1 = c0 + KV_RANK
    c2 = c1 + ROPE_DIM
    c3 = c2 + LRU_WIDTH
    w_in = lp['w_in']
    kr_cols = jnp.tile(w_in[:, c1:c2], (1, HEADS_PER_LANE_GROUP))
    w['w_in'] = jnp.concatenate([w_in[:, :c1], w_in[:, c2:c3], w_in[:, c3:], kr_cols], axis=1).astype(BF16)
    w['g_mix'] = lp['norm_mix'].reshape(1, d)
    w['g_q'] = lp['norm_q_lat'].reshape(1, Q_RANK)
    w['g_kv'] = lp['norm_kv_lat'].reshape(1, KV_RANK)
    wq = lp['w_q_b'].reshape(Q_RANK, N_HEADS, NOPE_DIM + ROPE_DIM)
    nope = wq[..., :NOPE_DIM].reshape(Q_RANK, N_HEADS * NOPE_DIM)
    x1 = wq[..., NOPE_DIM:NOPE_DIM + ROPE_HALF]
    x2 = wq[..., NOPE_DIM + ROPE_HALF:]
    rope_a = jnp.concatenate([x1, x2], axis=-1).reshape(Q_RANK, N_HEADS * ROPE_DIM)
    rope_b = jnp.concatenate([x2, x1], axis=-1).reshape(Q_RANK, N_HEADS * ROPE_DIM)
    w['w_q'] = jnp.concatenate([nope, rope_a, rope_b], axis=1).astype(BF16)
    wkv = lp['w_kv_b'].reshape(KV_RANK, N_HEADS, NOPE_DIM + V_DIM)
    w_uk = jnp.transpose(wkv[..., :NOPE_DIM], (1, 2, 0))
    w['w_uk'] = _block_diag(w_uk).astype(BF16)
    w_uv = jnp.transpose(wkv[..., NOPE_DIM:], (1, 0, 2))
    eye = jnp.eye(N_HEADS, dtype=F32)
    w['w_uv'] = jnp.einsum('hrv,hg->hrgv', w_uv, eye).reshape(N_HEADS, KV_RANK, N_HEADS * V_DIM).astype(BF16)
    w['g_att'] = lp['norm_att_out'].reshape(1, -1)
    w['conv_w'] = lp['conv_w']
    w['conv_b'] = lp['conv_b'].reshape(1, LRU_WIDTH)
    w['w_gx'] = _block_diag(lp['w_gate_x']).astype(BF16)
    w['b_gx'] = lp['b_gate_x'].reshape(1, LRU_WIDTH)
    w['w_ga'] = _block_diag(lp['w_gate_a']).astype(BF16)
    w['b_ga'] = lp['b_gate_a'].reshape(1, LRU_WIDTH)
    w['lam'] = lp['lru_lambda'].reshape(1, LRU_WIDTH)
    w['g_rec'] = lp['norm_rec_out'].reshape(1, LRU_WIDTH)
    n_att = N_HEADS * V_DIM
    w['w_o_att'] = lp['w_o'][:n_att].astype(BF16)
    w['w_o_rec'] = lp['w_o'][n_att:].astype(BF16)
    w['g_mem'] = lp['norm_mem_attn'].reshape(1, d)
    w['g_memtok'] = lp['norm_memtok'].reshape(1, d)
    w['w_mq'] = lp['w_mem_q'].astype(BF16)
    w['w_mkv'] = lp['w_mem_kv'].astype(BF16)
    w['w_mo'] = lp['w_mem_o'].astype(BF16)
    w['g_moe'] = lp['norm_moe'].reshape(1, d)
    w['w_router'] = jnp.pad(lp['w_router'], ((0, 0), (0, LANES - N_EXPERTS))).astype(BF16)
    w['b_router'] = jnp.pad(lp['b_router'], (0, LANES - N_EXPERTS), constant_values=NEG_BIG).reshape(1, LANES)
    w['w_gu'] = lp['w_gate_up']
    w['b_gu'] = lp['b_gate_up'].reshape(N_EXPERTS, 1, 2 * EXPERT_FF)
    w['w_dn'] = lp['w_down']
    w['b_dn'] = lp['b_down'].reshape(N_EXPERTS, 1, d)
    return w


def _rope_tables(pos, rows):
    inv = ROPE_THETA ** (-jnp.arange(ROPE_HALF, dtype=F32) / ROPE_HALF)
    ang = pos.astype(F32)[:, None] * inv[None, :]
    cos, sin = jnp.cos(ang), jnp.sin(ang)
    cos_t = jnp.tile(jnp.concatenate([cos, cos], axis=1), (1, HEADS_PER_LANE_GROUP))
    sin_t = jnp.tile(jnp.concatenate([-sin, sin], axis=1), (1, HEADS_PER_LANE_GROUP))
    reps = rows // pos.shape[0]
    return jnp.tile(cos_t, (reps, 1)), jnp.tile(sin_t, (reps, 1))


def _layer_prompt(x, mem, w):
    batch, seq, d = x.shape
    n = batch * seq
    x2d = x.reshape(n, d)
    cos_t, sin_t = _rope_tables(jnp.arange(seq), seq)
    q, _, kt, va, ckv, kr, xb, gb = _pre_call(x2d, cos_t, sin_t, w)
    att = _attn_prompt_call(q, kt, va, w, batch, seq)
    rec, conv_buf, h_last = _lru_prompt_call(xb, gb, w, batch, seq)
    x1, qm = _mix_out_call(x2d, att, rec, w)
    n_mem = mem.shape[1]
    mk, mv, mkb, mvb = _mem_kv_call(mem.reshape(batch * n_mem, d), w)
    o = _mem_attn_prompt_call(qm, mkb, mvb, batch, seq)
    routed = _router_call(x1, o, jnp.zeros((1, LANES), F32), w)
    caches = (ckv.reshape(batch, seq, KV_RANK), kr.reshape(batch, seq, ROPE_DIM), conv_buf,
              h_last.reshape(batch, LRU_WIDTH), mk.reshape(batch, n_mem, MEM_HEADS, MEM_HEAD_DIM),
              mv.reshape(batch, n_mem, MEM_HEADS, MEM_HEAD_DIM))
    return routed, caches


def _layer_sample(x, page_table, pool_ckv, pool_krope, conv_buf, h0, mk, mv, count0, w):
    batch, n_t, d = x.shape
    n = batch * n_t
    n_past = page_table.shape[1] * PAGE_SIZE
    x2d = x.reshape(n, d)
    cos_t, sin_t = _rope_tables(n_past + jnp.arange(n_t), min(TOKEN_TILE, n))
    q, kc, _, _, ckv, kr, xb, gb = _pre_call(x2d, cos_t, sin_t, w)
    q_item = jnp.transpose(q.reshape(N_HEADS, batch, n_t, QK_WIDTH), (1, 0, 2, 3))
    q_item = jnp.pad(q_item, ((0, 0), (0, 0), (0, TOK_PAD - n_t), (0, 0)))
    rope_parts = [q_item[:, h, :, KV_RANK + (h % HEADS_PER_LANE_GROUP) * ROPE_DIM:
                         KV_RANK + (h % HEADS_PER_LANE_GROUP + 1) * ROPE_DIM] for h in range(N_HEADS)]
    q_rope = jnp.stack(rope_parts, axis=1).reshape(batch, N_HEADS * TOK_PAD, ROPE_DIM)
    q_item = q_item.reshape(batch, N_HEADS * TOK_PAD, QK_WIDTH)
    att8 = _attn_sample_call(page_table, q_item, q_rope, kc.reshape(batch, n_t, QK_WIDTH), pool_ckv,
                             jnp.swapaxes(pool_krope, 1, 2), w)
    att = att8[:, :n_t].reshape(n, N_HEADS * V_DIM).astype(BF16)
    tmaj = lambda a: jnp.swapaxes(a, 0, 1)
    rec_t, conv_t, h_last = _lru_sample_call(tmaj(xb.reshape(batch, n_t, LRU_WIDTH)), tmaj(gb.reshape(batch, n_t, LRU_WIDTH)),
                                             tmaj(conv_buf), h0, w)
    rec = tmaj(rec_t).reshape(n, LRU_WIDTH)
    x1, qm = _mix_out_call(x2d, att, rec, w)
    q8 = jnp.pad(qm.reshape(batch, n_t, d).astype(F32), ((0, 0), (0, TOK_PAD - n_t), (0, 0)))
    o8 = _mem_attn_sample_call(q8, mk, mv)
    o = o8[:, :n_t].reshape(n, d).astype(BF16)
    routed = _router_call(x1, o, count0, w)
    caches = (ckv.reshape(batch, n_t, KV_RANK), kr.reshape(batch, n_t, ROPE_DIM), tmaj(conv_t), h_last)
    return routed, caches


def kernel(x_prompt, x_sample, mem_prompt, cache_ckv, cache_krope, cache_conv, state_rglru, cache_mem_k, cache_mem_v, page_table, w_in, norm_mix, norm_q_lat, w_q_b, norm_kv_lat, w_kv_b, conv_w, conv_b, w_gate_x, b_gate_x, w_gate_a, b_gate_a, lru_lambda, norm_att_out, norm_rec_out, w_o, norm_mem_attn, norm_memtok, w_mem_q, w_mem_kv, w_mem_o, norm_moe, w_router, b_router, w_gate_up, b_gate_up, w_down, b_down, norm_final):
    depth = w_in.shape[0]
    assert depth == 1, "the final norm is fused into the last layer's combine kernel"
    g_final = norm_final.reshape(1, -1)
    lp = dict(w_in=w_in[0], norm_mix=norm_mix[0], norm_q_lat=norm_q_lat[0], w_q_b=w_q_b[0],
              norm_kv_lat=norm_kv_lat[0], w_kv_b=w_kv_b[0], conv_w=conv_w[0], conv_b=conv_b[0],
              w_gate_x=w_gate_x[0], b_gate_x=b_gate_x[0], w_gate_a=w_gate_a[0], b_gate_a=b_gate_a[0],
              lru_lambda=lru_lambda[0], norm_att_out=norm_att_out[0], norm_rec_out=norm_rec_out[0],
              w_o=w_o[0], norm_mem_attn=norm_mem_attn[0], norm_memtok=norm_memtok[0],
              w_mem_q=w_mem_q[0], w_mem_kv=w_mem_kv[0], w_mem_o=w_mem_o[0], norm_moe=norm_moe[0],
              w_router=w_router[0], b_router=b_router[0], w_gate_up=w_gate_up[0],
              b_gate_up=b_gate_up[0], w_down=w_down[0], b_down=b_down[0])
    w = _prep_weights(lp)
    routed_p, (ckv_p, kr_p, conv_p, h_p, mk_p, mv_p) = _layer_prompt(x_prompt, mem_prompt, w)
    routed_s, (ckv_s, kr_s, conv_s, h_s) = _layer_sample(x_sample, page_table, cache_ckv[0], cache_krope[0],
                                                         cache_conv[0], state_rglru[0], cache_mem_k[0],
                                                         cache_mem_v[0], routed_p[5], w)
    yp, ys = _moe(routed_p, routed_s, routed_s[5], w, g_final)
    return (yp.reshape(x_prompt.shape), ys.reshape(x_sample.shape), ckv_p[None], kr_p[None], conv_p[None],
            h_p[None], mk_p[None], mv_p[None], ckv_s[None], kr_s[None], conv_s[None], h_s[None])
```
